```python
import math
import jax, jax.numpy as jnp
from jax import lax
import numpy as np

D_MODEL = 2048
BATCH = 1
SEQ = 8192
DEPTH = 4

S5_WIDTH = D_MODEL // 2
S5_GROUP = 16
S5_GROUPS = S5_WIDTH // S5_GROUP
S5_STATE = 64
S5_DT_MIN = 0.001
S5_DT_MAX = 0.1
DN_HEADS = 8
DN_DK = 128
DN_DV = 128
DN_QK_WIDTH = DN_HEADS * DN_DK
DN_V_WIDTH = DN_HEADS * DN_DV
DN_CONV = 4
DN_CHUNK = 64
DN_DT_MIN = 0.001
DN_DT_MAX = 0.1
FFN_DIM = 5632
FFN_CONV = 3
NORM_EPS = 1e-6

OFF_U = S5_WIDTH
OFF_QKV = OFF_U + 2 * DN_QK_WIDTH + DN_V_WIDTH
OFF_Z = OFF_QKV + DN_V_WIDTH
OFF_BETA = OFF_Z + DN_HEADS
OFF_ALPHA = OFF_BETA + DN_HEADS
OFF_GS = OFF_ALPHA + D_MODEL
N_IN = OFF_GS + D_MODEL
SPLITS = [OFF_U, OFF_QKV, OFF_Z, OFF_BETA, OFF_ALPHA, OFF_GS]

kernel_name = "hybrid_s5_gdn_convffn"


def rmsnorm(x, w):
    xf = x.astype(jnp.float32)
    y = xf * lax.rsqrt(jnp.mean(xf * xf, axis=-1, keepdims=True) + NORM_EPS) * w.astype(jnp.float32)
    return y.astype(x.dtype)


def l2norm(x):
    return x * lax.rsqrt(jnp.sum(x * x, axis=-1, keepdims=True) + NORM_EPS)


def causal_dwconv(x, w):
    K, C = w.shape
    return lax.conv_general_dilated(
        x, w[:, None, :].astype(x.dtype), window_strides=(1,), padding=[(K - 1, 0)],
        dimension_numbers=("NWC", "WIO", "NWC"), feature_group_count=C)


def s5_branch(u, log_dt, a_re, a_im, b_re, b_im, c_re, c_im, d):
    Bn, L, _ = u.shape
    f32 = jnp.float32
    uf = u.astype(f32)
    ug = uf.reshape(Bn, L, S5_GROUPS, S5_GROUP)
    lr, li = a_re.astype(f32), a_im.astype(f32)
    dt = jnp.exp(log_dt.astype(f32))[:, None]
    mag = jnp.exp(lr * dt)
    abar_re, abar_im = mag * jnp.cos(li * dt), mag * jnp.sin(li * dt)
    den = lr * lr + li * li
    nr, ni = abar_re - 1.0, abar_im
    coef_re = (nr * lr + ni * li) / den
    coef_im = (ni * lr - nr * li) / den
    br, bi = b_re.astype(f32), b_im.astype(f32)
    bbar_re = coef_re[..., None] * br - coef_im[..., None] * bi
    bbar_im = coef_re[..., None] * bi + coef_im[..., None] * br
    bu_re = jnp.einsum("gph,blgh->blgp", bbar_re, ug)
    bu_im = jnp.einsum("gph,blgh->blgp", bbar_im, ug)
    ar_t = jnp.broadcast_to(abar_re, bu_re.shape)
    ai_t = jnp.broadcast_to(abar_im, bu_re.shape)

    def combine(e1, e2):
        a1r, a1i, b1r, b1i = e1
        a2r, a2i, b2r, b2i = e2
        return (a2r * a1r - a2i * a1i,
                a2r * a1i + a2i * a1r,
                a2r * b1r - a2i * b1i + b2r,
                a2r * b1i + a2i * b1r + b2i)

    _, _, xr, xi = lax.associative_scan(combine, (ar_t, ai_t, bu_re, bu_im), axis=1)
    y = (jnp.einsum("ghp,blgp->blgh", c_re.astype(f32), xr)
         - jnp.einsum("ghp,blgp->blgh", c_im.astype(f32), xi))
    y = y.reshape(Bn, L, S5_WIDTH) + d.astype(f32) * uf
    return jax.nn.gelu(y, approximate=False).astype(u.dtype)


def chunk_gated_delta_rule(q, k, v, g, beta):
    Bn, L, H, DK = q.shape
    DV = v.shape[-1]
    N, C = L // DN_CHUNK, DN_CHUNK

    def chunks(t):
        return t.reshape(Bn, N, C, H, -1).transpose(1, 0, 3, 2, 4)

    qc, kc, vc = chunks(q), chunks(k), chunks(v)
    bc = beta.reshape(Bn, N, C, H).transpose(1, 0, 3, 2)
    gc = jnp.cumsum(g.reshape(Bn, N, C, H).transpose(1, 0, 3, 2), axis=-1)
    tril = jnp.tril(jnp.ones((C, C), dtype=bool))
    strict = jnp.tril(jnp.ones((C, C), dtype=bool), -1)
    decay = jnp.exp(jnp.where(tril, gc[..., :, None] - gc[..., None, :], -jnp.inf))
    k_beta = kc * bc[..., None]
    v_beta = vc * bc[..., None]
    lmat = jnp.where(strict, jnp.einsum("nbhcd,nbhsd->nbhcs", k_beta, kc) * decay, 0.0)
    eye = jnp.eye(C, dtype=q.dtype)
    rhs = jnp.concatenate([v_beta, k_beta * jnp.exp(gc)[..., None]], axis=-1)
    sol = lax.linalg.triangular_solve(eye + lmat, rhs, left_side=True, lower=True, unit_diagonal=True)
    u_c, w_c = sol[..., :DV], sol[..., DV:]
    attn = jnp.where(tril, jnp.einsum("nbhcd,nbhsd->nbhcs", qc, kc) * decay, 0.0)

    def step(S, inp):
        q_i, k_i, u_i, w_i, g_i, a_i = inp
        v_new = u_i - jnp.einsum("bhcd,bhdv->bhcv", w_i, S)
        o_i = (jnp.einsum("bhcd,bhdv->bhcv", q_i * jnp.exp(g_i)[..., None], S)
               + jnp.einsum("bhcs,bhsv->bhcv", a_i, v_new))
        g_last = g_i[..., -1]
        S = (S * jnp.exp(g_last)[..., None, None]
             + jnp.einsum("bhcd,bhcv->bhdv", k_i * jnp.exp(g_last[..., None] - g_i)[..., None], v_new))
        return S, o_i

    S0 = jnp.zeros((Bn, H, DK, DV), q.dtype)
    _, o = lax.scan(step, S0, (qc, kc, u_c, w_c, gc, attn))
    return o.transpose(1, 0, 3, 2, 4).reshape(Bn, L, H, DV)


def deltanet_branch(qkv, z, b_raw, a_raw, conv_w, a_log, dt_bias, norm_w):
    Bn, L, _ = qkv.shape
    f32 = jnp.float32
    qkv = jax.nn.silu(causal_dwconv(qkv, conv_w))
    q, k, v = jnp.split(qkv, [DN_QK_WIDTH, 2 * DN_QK_WIDTH], axis=-1)
    q = l2norm(q.reshape(Bn, L, DN_HEADS, DN_DK).astype(f32)) * (DN_DK ** -0.5)
    k = l2norm(k.reshape(Bn, L, DN_HEADS, DN_DK).astype(f32))
    v = v.reshape(Bn, L, DN_HEADS, DN_DV).astype(f32)
    beta = jax.nn.sigmoid(b_raw.astype(f32))
    g = -jnp.exp(a_log.astype(f32)) * jax.nn.softplus(a_raw.astype(f32) + dt_bias.astype(f32))
    o = chunk_gated_delta_rule(q, k, v, g, beta)
    o = rmsnorm(o, norm_w) * jax.nn.silu(z.reshape(Bn, L, DN_HEADS, DN_DV).astype(f32))
    return o.reshape(Bn, L, DN_V_WIDTH).astype(qkv.dtype)


def setup_inputs(seed: int = 0) -> dict:
    key = jax.random.key(seed)
    ks = jax.random.split(key, 26)
    f32 = jnp.float32

    def nrm(k, shape, scale):
        return jax.random.normal(k, shape, f32) * scale

    G, P, HG = S5_GROUPS, S5_STATE, S5_GROUP
    x = nrm(ks[0], (BATCH, SEQ, D_MODEL), 1.0)
    mix_norm_w = 1.0 + nrm(ks[1], (DEPTH, D_MODEL), 0.02)
    w_in = nrm(ks[2], (DEPTH, D_MODEL, N_IN), D_MODEL ** -0.5)
    s5_log_dt = jax.random.uniform(ks[3], (DEPTH, G), f32, math.log(S5_DT_MIN), math.log(S5_DT_MAX))
    s5_a_re = -0.5 + nrm(ks[4], (DEPTH, G, P), 0.01)
    s5_a_im = math.pi * jnp.arange(P, dtype=f32) + nrm(ks[5], (DEPTH, G, P), 0.01)
    s5_b_re = nrm(ks[6], (DEPTH, G, P, HG), (2 * HG) ** -0.5)
    s5_b_im = nrm(ks[7], (DEPTH, G, P, HG), (2 * HG) ** -0.5)
    s5_c_re = nrm(ks[8], (DEPTH, G, HG, P), (2 * P) ** -0.5)
    s5_c_im = nrm(ks[9], (DEPTH, G, HG, P), (2 * P) ** -0.5)
    s5_d = nrm(ks[10], (DEPTH, S5_WIDTH), 1.0)
    s5_glu_w = nrm(ks[11], (DEPTH, S5_WIDTH, 2 * D_MODEL), S5_WIDTH ** -0.5)
    dn_conv_w = nrm(ks[12], (DEPTH, DN_CONV, 2 * DN_QK_WIDTH + DN_V_WIDTH), DN_CONV ** -0.5)
    dn_a_log = jnp.log(jax.random.uniform(ks[13], (DEPTH, DN_HEADS), f32, 1.0, 16.0))
    dn_dt = jnp.exp(jax.random.uniform(ks[14], (DEPTH, DN_HEADS), f32, math.log(DN_DT_MIN), math.log(DN_DT_MAX)))
    dn_dt_bias = dn_dt + jnp.log(-jnp.expm1(-dn_dt))
    dn_norm_w = 1.0 + nrm(ks[15], (DEPTH, DN_DV), 0.02)
    dn_proj_w = nrm(ks[16], (DEPTH, DN_V_WIDTH, D_MODEL), DN_V_WIDTH ** -0.5)
    w_out = nrm(ks[17], (DEPTH, D_MODEL, D_MODEL), D_MODEL ** -0.5)
    ffn_norm_w = 1.0 + nrm(ks[18], (DEPTH, D_MODEL), 0.02)
    ffn_up = nrm(ks[19], (DEPTH, D_MODEL, 2 * FFN_DIM), D_MODEL ** -0.5)
    ffn_conv_w = nrm(ks[20], (DEPTH, FFN_CONV, 2 * FFN_DIM), FFN_CONV ** -0.5)
    ffn_down = nrm(ks[21], (DEPTH, FFN_DIM, D_MODEL), FFN_DIM ** -0.5)
    final_norm_w = 1.0 + nrm(ks[22], (D_MODEL,), 0.02)
    return {"x": x, "mix_norm_w": mix_norm_w, "w_in": w_in, "s5_log_dt": s5_log_dt,
            "s5_a_re": s5_a_re, "s5_a_im": s5_a_im, "s5_b_re": s5_b_re, "s5_b_im": s5_b_im,
            "s5_c_re": s5_c_re, "s5_c_im": s5_c_im, "s5_d": s5_d, "s5_glu_w": s5_glu_w,
            "dn_conv_w": dn_conv_w, "dn_a_log": dn_a_log, "dn_dt_bias": dn_dt_bias,
            "dn_norm_w": dn_norm_w, "dn_proj_w": dn_proj_w, "w_out": w_out,
            "ffn_norm_w": ffn_norm_w, "ffn_up": ffn_up, "ffn_conv_w": ffn_conv_w,
            "ffn_down": ffn_down, "final_norm_w": final_norm_w}


def reference(x, mix_norm_w, w_in, s5_log_dt, s5_a_re, s5_a_im, s5_b_re, s5_b_im,
              s5_c_re, s5_c_im, s5_d, s5_glu_w, dn_conv_w, dn_a_log, dn_dt_bias,
              dn_norm_w, dn_proj_w, w_out, ffn_norm_w, ffn_up, ffn_conv_w, ffn_down,
              final_norm_w):
    for l in range(DEPTH):
        h = rmsnorm(x, mix_norm_w[l])
        proj = jnp.einsum("bld,de->ble", h, w_in[l])
        u, qkv, z, b_raw, a_raw, g_s5, g_dn = jnp.split(proj, SPLITS, axis=-1)
        y_s5 = s5_branch(u, s5_log_dt[l], s5_a_re[l], s5_a_im[l], s5_b_re[l], s5_b_im[l],
                         s5_c_re[l], s5_c_im[l], s5_d[l])
        glu_a, glu_b = jnp.split(jnp.einsum("blc,ce->ble", y_s5, s5_glu_w[l]), 2, axis=-1)
        br_s5 = glu_a * jax.nn.sigmoid(glu_b)
        y_dn = deltanet_branch(qkv, z, b_raw, a_raw, dn_conv_w[l], dn_a_log[l], dn_dt_bias[l], dn_norm_w[l])
        br_dn = jnp.einsum("blc,cd->bld", y_dn, dn_proj_w[l])
        merged = jax.nn.sigmoid(g_s5) * br_s5 + jax.nn.sigmoid(g_dn) * br_dn
        x = x + jnp.einsum("bld,de->ble", merged, w_out[l])
        h = rmsnorm(x, ffn_norm_w[l])
        up = causal_dwconv(jnp.einsum("bld,df->blf", h, ffn_up[l]), ffn_conv_w[l])
        act, val = jnp.split(up, 2, axis=-1)
        x = x + jnp.einsum("blf,fd->bld", jax.nn.silu(act) * val, ffn_down[l])
    return rmsnorm(x, final_norm_w)
```

```python
import functools
import math

import jax
import jax.numpy as jnp
from jax import lax
from jax.experimental import pallas as pl
from jax.experimental.pallas import tpu as pltpu

F32 = jnp.float32
BF16 = jnp.bfloat16

D_MODEL = 2048
S5_WIDTH = 1024
S5_GROUP = 16
S5_GROUPS = 64
S5_STATE = 64
DN_HEADS = 8
DN_DK = 128
DN_DV = 128
DN_QK_WIDTH = DN_HEADS * DN_DK
DN_V_WIDTH = DN_HEADS * DN_DV
DN_CONV = 4
DN_CHUNK = 64
FFN_DIM = 5632
FFN_CONV = 3
NORM_EPS = 1e-6

OFF_U = S5_WIDTH
OFF_QKV = OFF_U + 2 * DN_QK_WIDTH + DN_V_WIDTH
OFF_Z = OFF_QKV + DN_V_WIDTH
OFF_BETA = OFF_Z + DN_HEADS
OFF_ALPHA = OFF_BETA + DN_HEADS
OFF_GS = OFF_ALPHA + D_MODEL
N_IN = OFF_GS + D_MODEL
N_MAIN = N_IN - 2 * DN_HEADS

LANES = 128
SUBLANES = 8
VMEM_LIMIT = 56 * 1024 * 1024

S5_SEGS = SUBLANES
S5_GB = 8
S5_NBLK = S5_GROUPS // S5_GB
S5_BLK_CH = S5_GB * S5_GROUP
S5_BLK_ST = S5_GB * S5_STATE

HI = lax.Precision.HIGHEST


def _cparams(sem):
    return pltpu.CompilerParams(dimension_semantics=sem, vmem_limit_bytes=VMEM_LIMIT)


def _sigmoid(x):
    return 1.0 / (1.0 + jnp.exp(-x))


def _silu(x):
    return x * _sigmoid(x)


def _softplus(x):
    return jnp.maximum(x, 0.0) + jnp.log1p(jnp.exp(-jnp.abs(x)))


def _rms_rows(x, w):
    ms = jnp.mean(x * x, axis=-1, keepdims=True)
    return x * lax.rsqrt(ms + NORM_EPS) * w


def _rmsnorm_kernel(x_ref, w_ref, o_ref):
    o_ref[...] = _rms_rows(x_ref[...], w_ref[...]).astype(o_ref.dtype)


def rmsnorm_cast(x, w, tm=512):
    L, D = x.shape
    return pl.pallas_call(
        _rmsnorm_kernel,
        grid=(L // tm,),
        in_specs=[pl.BlockSpec((tm, D), lambda m: (m, 0)),
                  pl.BlockSpec((1, D), lambda m: (0, 0))],
        out_specs=pl.BlockSpec((tm, D), lambda m: (m, 0)),
        out_shape=jax.ShapeDtypeStruct((L, D), BF16),
        compiler_params=_cparams(("parallel",)),
        name="rmsnorm",
    )(x, w.reshape(1, D))


def _in_proj_kernel(h_ref, w_ref, wba_ref, o_ref, ba_ref):
    h = h_ref[...]
    o_ref[...] = jnp.dot(h, w_ref[...], preferred_element_type=F32)

    @pl.when(pl.program_id(1) == 0)
    def _():
        ba_ref[...] = jnp.dot(h, wba_ref[...], preferred_element_type=F32)


def in_proj(h, w_main, w_ba, tm=1024, tn=1024):
    L, D = h.shape
    N = w_main.shape[1]
    return pl.pallas_call(
        _in_proj_kernel,
        grid=(L // tm, N // tn),
        in_specs=[pl.BlockSpec((tm, D), lambda m, n: (m, 0)),
                  pl.BlockSpec((D, tn), lambda m, n: (0, n)),
                  pl.BlockSpec((D, LANES), lambda m, n: (0, 0))],
        out_specs=[pl.BlockSpec((tm, tn), lambda m, n: (m, n)),
                   pl.BlockSpec((tm, LANES), lambda m, n: (m, 0))],
        out_shape=[jax.ShapeDtypeStruct((L, N), F32),
                   jax.ShapeDtypeStruct((L, LANES), F32)],
        compiler_params=_cparams(("parallel", "arbitrary")),
        name="in_proj",
    )(h, w_main, w_ba)


def _s5_disc_kernel(ldt_ref, are_ref, aim_ref, bre_ref, bim_ref,
                    abr_ref, abi_ref, bbr_ref, bbi_ref):
    lr = are_ref[...]
    li = aim_ref[...]
    dt = jnp.exp(ldt_ref[...])
    mag = jnp.exp(lr * dt)
    abar_re = mag * jnp.cos(li * dt)
    abar_im = mag * jnp.sin(li * dt)
    den = lr * lr + li * li
    nr = abar_re - 1.0
    ni = abar_im
    coef_re = (nr * lr + ni * li) / den
    coef_im = (ni * lr - nr * li) / den
    br = bre_ref[...]
    bi = bim_ref[...]
    abr_ref[...] = abar_re
    abi_ref[...] = abar_im
    bbr_ref[...] = coef_re * br - coef_im * bi
    bbi_ref[...] = coef_re * bi + coef_im * br


def s5_discretise(log_dt, a_re, a_im, b_re, b_im):
    G, P, HG = S5_GROUPS, S5_STATE, S5_GROUP
    bt_re = jnp.transpose(b_re, (0, 2, 1))
    bt_im = jnp.transpose(b_im, (0, 2, 1))
    return pl.pallas_call(
        _s5_disc_kernel,
        out_shape=[jax.ShapeDtypeStruct((G, 1, P), F32),
                   jax.ShapeDtypeStruct((G, 1, P), F32),
                   jax.ShapeDtypeStruct((G, HG, P), F32),
                   jax.ShapeDtypeStruct((G, HG, P), F32)],
        name="s5_disc",
    )(jnp.broadcast_to(log_dt.reshape(G, 1, 1), (G, 1, P)), a_re.reshape(G, 1, P), a_im.reshape(G, 1, P),
      bt_re, bt_im)


def _block_diag(t):
    nb, gb, r, c = t.shape
    eye = jnp.eye(gb, dtype=t.dtype)
    out = t[:, :, :, None, :] * eye[None, :, None, :, None]
    return out.reshape(nb, gb * r, gb * c)


def s5_layout_params(abar_re, abar_im, bbar_re, bbar_im, c_re, c_im):
    P, HG = S5_STATE, S5_GROUP
    a_r = jnp.broadcast_to(abar_re.reshape(1, S5_GROUPS * P), (S5_SEGS, S5_GROUPS * P))
    a_i = jnp.broadcast_to(abar_im.reshape(1, S5_GROUPS * P), (S5_SEGS, S5_GROUPS * P))
    wb = jnp.concatenate(
        [_block_diag(bbar_re.reshape(S5_NBLK, S5_GB, HG, P)),
         _block_diag(bbar_im.reshape(S5_NBLK, S5_GB, HG, P))], axis=-1).astype(BF16)
    ct_re = jnp.transpose(c_re, (0, 2, 1)).reshape(S5_NBLK, S5_GB, P, HG)
    ct_im = jnp.transpose(c_im, (0, 2, 1)).reshape(S5_NBLK, S5_GB, P, HG)
    wc = jnp.concatenate([_block_diag(ct_re), -_block_diag(ct_im)], axis=1).astype(BF16)
    return a_r, a_i, wb, wc


def _s5_kernel(u_ref, p_ref, pt_ref, wb_ref, ar_ref, ai_ref, wc_ref, d_ref, y_ref,
               bu_ref, x_ref, sr_ref, si_ref, *, ti, seg_len):
    R = S5_SEGS * ti
    NS = S5_BLK_ST
    ph = pl.program_id(0)
    it = pl.program_id(1)

    @pl.when(jnp.logical_and(ph == 0, it == 0))
    def _():
        sr_ref[...] = jnp.zeros_like(sr_ref)
        si_ref[...] = jnp.zeros_like(si_ref)

    @pl.when(jnp.logical_and(ph == 1, it == 0))
    def _():
        pr, pi = ar_ref[...], ai_ref[...]
        for _ in range(int(math.log2(seg_len))):
            pr, pi = pr * pr - pi * pi, 2.0 * pr * pi
        er, ei = sr_ref[...], si_ref[...]
        row = lax.broadcasted_iota(jnp.int32, er.shape, 0)
        zr = jnp.zeros_like(er)
        zi = jnp.zeros_like(ei)
        for _ in range(S5_SEGS - 1):
            nr = pr * zr - pi * zi + er
            ni = pr * zi + pi * zr + ei
            zr = jnp.where(row >= 1, pltpu.roll(nr, 1, 0), 0.0)
            zi = jnp.where(row >= 1, pltpu.roll(ni, 1, 0), 0.0)
        sr_ref[...] = zr
        si_ref[...] = zi

    u_nat = u_ref[...].reshape(R, S5_WIDTH)
    u2 = jnp.dot(p_ref[...], u_nat.astype(BF16), preferred_element_type=F32).astype(BF16)
    for b in range(S5_NBLK):
        bu_ref[:, b * 2 * NS:(b + 1) * 2 * NS] = jnp.dot(
            u2[:, b * S5_BLK_CH:(b + 1) * S5_BLK_CH], wb_ref[b], preferred_element_type=F32)

    def scan(store):
        for b in range(S5_NBLK):
            ar = ar_ref[:, b * NS:(b + 1) * NS]
            ai = ai_ref[:, b * NS:(b + 1) * NS]
            c_re = b * 2 * NS
            c_im = c_re + NS

            def body(i, carry):
                xr, xi = carry
                r0 = pl.multiple_of(i * SUBLANES, SUBLANES)
                br = bu_ref[pl.ds(r0, SUBLANES), c_re:c_re + NS]
                bi = bu_ref[pl.ds(r0, SUBLANES), c_im:c_im + NS]
                nr = ar * xr - ai * xi + br
                ni = ar * xi + ai * xr + bi
                if store:
                    x_ref[pl.ds(r0, SUBLANES), c_re:c_re + NS] = nr
                    x_ref[pl.ds(r0, SUBLANES), c_im:c_im + NS] = ni
                return nr, ni

            xr, xi = lax.fori_loop(
                0, ti, body, (sr_ref[:, b * NS:(b + 1) * NS], si_ref[:, b * NS:(b + 1) * NS]), unroll=4)
            sr_ref[:, b * NS:(b + 1) * NS] = xr
            si_ref[:, b * NS:(b + 1) * NS] = xi

    @pl.when(ph == 0)
    def _():
        scan(False)

    @pl.when(ph == 1)
    def _():
        scan(True)
        ys = []
        for b in range(S5_NBLK):
            xb = x_ref[:, b * 2 * NS:(b + 1) * 2 * NS].astype(BF16)
            ys.append(jnp.dot(xb, wc_ref[b], preferred_element_type=F32))
        y2 = jnp.concatenate(ys, axis=1)
        y_hi = y2.astype(BF16)
        y_lo = (y2 - y_hi.astype(F32)).astype(BF16)
        pt = pt_ref[...]
        y_nat = (jnp.dot(pt, y_hi, preferred_element_type=F32)
                 + jnp.dot(pt, y_lo, preferred_element_type=F32))
        y_nat = y_nat + d_ref[...] * u_nat
        y = 0.5 * y_nat * (1.0 + lax.erf(y_nat * (1.0 / math.sqrt(2.0))))
        y_ref[...] = y.reshape(S5_SEGS, ti, S5_WIDTH).astype(y_ref.dtype)


def s5_scan(proj, a_r, a_i, wb, wc, d, ti=32):
    L = proj.shape[0]
    seg_len = L // S5_SEGS
    assert seg_len * S5_SEGS == L and seg_len % ti == 0 and (seg_len & (seg_len - 1)) == 0
    R = S5_SEGS * ti
    nt = seg_len // ti
    proj3 = proj.reshape(S5_SEGS, seg_len, proj.shape[1])
    r = jnp.arange(R)
    perm = jnp.zeros((R, R), BF16).at[r, (r % S5_SEGS) * ti + r // S5_SEGS].set(1)
    perm_t = perm.T
    ns = S5_GROUPS * S5_STATE
    kern = functools.partial(_s5_kernel, ti=ti, seg_len=seg_len)
    y = pl.pallas_call(
        kern,
        grid=(2, nt),
        in_specs=[pl.BlockSpec((S5_SEGS, ti, S5_WIDTH), lambda p, t: (0, t, 0)),
                  pl.BlockSpec((R, R), lambda p, t: (0, 0)),
                  pl.BlockSpec((R, R), lambda p, t: (0, 0)),
                  pl.BlockSpec((S5_NBLK, S5_BLK_CH, 2 * S5_BLK_ST), lambda p, t: (0, 0, 0)),
                  pl.BlockSpec((S5_SEGS, ns), lambda p, t: (0, 0)),
                  pl.BlockSpec((S5_SEGS, ns), lambda p, t: (0, 0)),
                  pl.BlockSpec((S5_NBLK, 2 * S5_BLK_ST, S5_BLK_CH), lambda p, t: (0, 0, 0)),
                  pl.BlockSpec((1, S5_WIDTH), lambda p, t: (0, 0))],
        out_specs=pl.BlockSpec((S5_SEGS, ti, S5_WIDTH), lambda p, t: (0, t * p, 0)),
        out_shape=jax.ShapeDtypeStruct((S5_SEGS, seg_len, S5_WIDTH), BF16),
        scratch_shapes=[pltpu.VMEM((R, 2 * ns), F32),
                        pltpu.VMEM((R, 2 * ns), F32),
                        pltpu.VMEM((S5_SEGS, ns), F32),
                        pltpu.VMEM((S5_SEGS, ns), F32)],
        compiler_params=_cparams(("arbitrary", "arbitrary")),
        name="s5_scan",
    )(proj3, perm, perm_t, wb, a_r, a_i, wc, d.reshape(1, S5_WIDTH))
    return y.reshape(L, S5_WIDTH)


def _dot_nt(a, b):
    return lax.dot_general(a, b, (((1,), (1,)), ((), ())), preferred_element_type=F32)


def _dot_tn(a, b, precision=None):
    return lax.dot_general(a, b, (((0,), (0,)), ((), ())), preferred_element_type=F32,
                           precision=precision)


def _dn_kernel(q_ref, k_ref, v_ref, z_ref, ba_ref, cw_ref, alog_ref, dtb_ref, nw_ref, o_ref,
               ext_ref, act_ref, gate_ref, s_ref, *, tm):
    C = DN_CHUNK
    W = DN_QK_WIDTH
    it = pl.program_id(0)

    @pl.when(it == 0)
    def _():
        ext_ref[0:SUBLANES, :] = jnp.zeros((SUBLANES, 3 * W), F32)
        s_ref[...] = jnp.zeros_like(s_ref)

    ext_ref[SUBLANES:SUBLANES + tm, 0:W] = q_ref[...]
    ext_ref[SUBLANES:SUBLANES + tm, W:2 * W] = k_ref[...]
    ext_ref[SUBLANES:SUBLANES + tm, 2 * W:3 * W] = v_ref[...]
    cw = cw_ref[...]
    acc = None
    for j in range(DN_CONV):
        off = SUBLANES - (DN_CONV - 1) + j
        term = cw[j:j + 1, :] * ext_ref[off:off + tm, :]
        acc = term if acc is None else acc + term
    act_ref[...] = _silu(acc)
    ext_ref[0:SUBLANES, :] = ext_ref[tm:tm + SUBLANES, :]

    ba = ba_ref[...]
    lane = lax.broadcasted_iota(jnp.int32, ba.shape, 1)
    beta = _sigmoid(ba)
    g = -jnp.exp(alog_ref[...]) * _softplus(ba + dtb_ref[...])
    gate_ref[...] = jnp.where(lane < DN_HEADS, beta, g)

    ri = lax.broadcasted_iota(jnp.int32, (C, C), 0)
    ci = lax.broadcasted_iota(jnp.int32, (C, C), 1)
    tril = ri >= ci
    strict = ri > ci
    tril_f = tril.astype(F32)
    triu_f = (ri <= ci).astype(F32)
    eye_f = (ri == ci).astype(F32)
    nw = nw_ref[...]

    def chunk(c, carry):
        r0 = pl.multiple_of(c * C, C)
        gt = gate_ref[pl.ds(r0, C), :]
        gc = jnp.dot(tril_f, gt, preferred_element_type=F32, precision=HI)
        gc_t = _dot_tn(gt, triu_f, precision=HI)
        for h in range(DN_HEADS):
            lo = h * DN_DK
            qh = act_ref[pl.ds(r0, C), lo:lo + DN_DK]
            kh = act_ref[pl.ds(r0, C), W + lo:W + lo + DN_DK]
            vh = act_ref[pl.ds(r0, C), 2 * W + lo:2 * W + lo + DN_DV]
            qh = qh * lax.rsqrt(jnp.sum(qh * qh, axis=-1, keepdims=True) + NORM_EPS) * (DN_DK ** -0.5)
            kh = kh * lax.rsqrt(jnp.sum(kh * kh, axis=-1, keepdims=True) + NORM_EPS)
            bcol = gt[:, h:h + 1]
            gcol = gc[:, DN_HEADS + h:DN_HEADS + h + 1]
            grow = gc_t[DN_HEADS + h:DN_HEADS + h + 1, :]
            glast = gcol[C - 1:C, :]
            dec = jnp.where(tril, jnp.exp(jnp.where(tril, gcol - grow, 0.0)), 0.0)
            kb = kh * bcol
            vb = vh * bcol
            kb16 = kb.astype(BF16)
            kh16 = kh.astype(BF16)
            lmat = jnp.where(strict, _dot_nt(kb16, kh16) * dec, 0.0)
            attn = _dot_nt(qh.astype(BF16), kh16) * dec
            pw = -lmat
            tinv = eye_f + pw
            for _ in range(int(math.log2(C)) - 1):
                pw = jnp.dot(pw, pw, preferred_element_type=F32, precision=HI)
                tinv = tinv + jnp.dot(tinv, pw, preferred_element_type=F32, precision=HI)
            eg = jnp.exp(gcol)
            rhs = jnp.concatenate([vb, kb * eg], axis=1).astype(BF16)
            sol = jnp.dot(tinv.astype(BF16), rhs, preferred_element_type=F32)
            u_c = sol[:, :DN_DV]
            w_c = sol[:, DN_DV:]
            s = s_ref[h]
            s16 = s.astype(BF16)
            v_new = u_c - jnp.dot(w_c.astype(BF16), s16, preferred_element_type=F32)
            vn16 = v_new.astype(BF16)
            o = (jnp.dot((qh * eg).astype(BF16), s16, preferred_element_type=F32)
                 + jnp.dot(attn.astype(BF16), vn16, preferred_element_type=F32))
            kd = (kh * jnp.exp(glast - gcol)).astype(BF16)
            s_ref[h] = s * jnp.exp(glast) + _dot_tn(kd, vn16)
            zh = z_ref[pl.ds(r0, C), lo:lo + DN_DV]
            on = o * lax.rsqrt(jnp.mean(o * o, axis=-1, keepdims=True) + NORM_EPS) * nw
            o_ref[pl.ds(r0, C), lo:lo + DN_DV] = (on * _silu(zh)).astype(o_ref.dtype)
        return carry

    lax.fori_loop(0, tm // C, chunk, 0)


def deltanet(proj, ba, conv_w, a_log, dt_bias, norm_w, tm=256):
    L = proj.shape[0]
    W = DN_QK_WIDTH
    pad = jnp.zeros((1, LANES), F32)
    alog_row = pad.at[0, DN_HEADS:2 * DN_HEADS].set(a_log)
    dtb_row = pad.at[0, DN_HEADS:2 * DN_HEADS].set(dt_bias)
    kern = functools.partial(_dn_kernel, tm=tm)
    return pl.pallas_call(
        kern,
        grid=(L // tm,),
        in_specs=[pl.BlockSpec((tm, W), lambda t: (t, 1)),
                  pl.BlockSpec((tm, W), lambda t: (t, 2)),
                  pl.BlockSpec((tm, W), lambda t: (t, 3)),
                  pl.BlockSpec((tm, W), lambda t: (t, 4)),
                  pl.BlockSpec((tm, LANES), lambda t: (t, 0)),
                  pl.BlockSpec((DN_CONV, 3 * W), lambda t: (0, 0)),
                  pl.BlockSpec((1, LANES), lambda t: (0, 0)),
                  pl.BlockSpec((1, LANES), lambda t: (0, 0)),
                  pl.BlockSpec((1, DN_DV), lambda t: (0, 0))],
        out_specs=pl.BlockSpec((tm, DN_V_WIDTH), lambda t: (t, 0)),
        out_shape=jax.ShapeDtypeStruct((L, DN_V_WIDTH), BF16),
        scratch_shapes=[pltpu.VMEM((tm + SUBLANES, 3 * W), F32),
                        pltpu.VMEM((tm, 3 * W), F32),
                        pltpu.VMEM((tm, LANES), F32),
                        pltpu.VMEM((DN_HEADS, DN_DK, DN_DV), F32)],
        compiler_params=_cparams(("arbitrary",)),
        name="deltanet",
    )(proj, proj, proj, proj, ba, conv_w, alog_row, dtb_row, norm_w.reshape(1, DN_DV))


def _merge_kernel(ys_ref, yd_ref, wga_ref, wgb_ref, wdn_ref, gs_ref, gd_ref, o_ref):
    ys = ys_ref[...]
    ga = jnp.dot(ys, wga_ref[...], preferred_element_type=F32)
    gb = jnp.dot(ys, wgb_ref[...], preferred_element_type=F32)
    dn = jnp.dot(yd_ref[...], wdn_ref[...], preferred_element_type=F32)
    br_s5 = ga * _sigmoid(gb)
    o_ref[...] = (_sigmoid(gs_ref[...]) * br_s5 + _sigmoid(gd_ref[...]) * dn).astype(o_ref.dtype)


def merge(y_s5, y_dn, glu_w, dn_w, proj, tm=1024, tn=512):
    L = y_s5.shape[0]
    nb = D_MODEL // tn
    gs_blk = (OFF_Z) // tn
    gd_blk = gs_blk + nb
    return pl.pallas_call(
        _merge_kernel,
        grid=(L // tm, nb),
        in_specs=[pl.BlockSpec((tm, S5_WIDTH), lambda m, n: (m, 0)),
                  pl.BlockSpec((tm, DN_V_WIDTH), lambda m, n: (m, 0)),
                  pl.BlockSpec((S5_WIDTH, tn), lambda m, n: (0, n)),
                  pl.BlockSpec((S5_WIDTH, tn), lambda m, n: (0, n + nb)),
                  pl.BlockSpec((DN_V_WIDTH, tn), lambda m, n: (0, n)),
                  pl.BlockSpec((tm, tn), lambda m, n: (m, n + gs_blk)),
                  pl.BlockSpec((tm, tn), lambda m, n: (m, n + gd_blk))],
        out_specs=pl.BlockSpec((tm, tn), lambda m, n: (m, n)),
        out_shape=jax.ShapeDtypeStruct((L, D_MODEL), BF16),
        compiler_params=_cparams(("parallel", "arbitrary")),
        name="merge",
    )(y_s5, y_dn, glu_w, glu_w, dn_w, proj, proj)


def _out_proj_kernel(m_ref, w_ref, x_ref, nw_ref, xo_ref, h_ref):
    xo = x_ref[...] + jnp.dot(m_ref[...], w_ref[...], preferred_element_type=F32)
    xo_ref[...] = xo
    h_ref[...] = _rms_rows(xo, nw_ref[...]).astype(h_ref.dtype)


def out_proj(merged, w_out, x, norm_w, tm=512):
    L, D = x.shape
    return pl.pallas_call(
        _out_proj_kernel,
        grid=(L // tm,),
        in_specs=[pl.BlockSpec((tm, D), lambda m: (m, 0)),
                  pl.BlockSpec((D, D), lambda m: (0, 0), pipeline_mode=pl.Buffered(1)),
                  pl.BlockSpec((tm, D), lambda m: (m, 0)),
                  pl.BlockSpec((1, D), lambda m: (0, 0))],
        out_specs=[pl.BlockSpec((tm, D), lambda m: (m, 0)),
                   pl.BlockSpec((tm, D), lambda m: (m, 0))],
        out_shape=[jax.ShapeDtypeStruct((L, D), F32),
                   jax.ShapeDtypeStruct((L, D), BF16)],
        compiler_params=_cparams(("parallel",)),
        name="out_proj",
    )(merged, w_out, x, norm_w.reshape(1, D))


def _ffn_up_kernel(h_ref, wa_ref, wv_ref, cwa_ref, cwv_ref, o_ref, ext_ref, *, tm, tn):
    @pl.when(pl.program_id(1) == 0)
    def _():
        ext_ref[0:SUBLANES, :] = jnp.zeros((SUBLANES, 2 * tn), F32)

    h = h_ref[...]
    ext_ref[SUBLANES:SUBLANES + tm, 0:tn] = jnp.dot(h, wa_ref[...], preferred_element_type=F32)
    ext_ref[SUBLANES:SUBLANES + tm, tn:2 * tn] = jnp.dot(h, wv_ref[...], preferred_element_type=F32)
    cw = jnp.concatenate([cwa_ref[...], cwv_ref[...]], axis=1)
    acc = None
    for j in range(FFN_CONV):
        off = SUBLANES - (FFN_CONV - 1) + j
        term = cw[j:j + 1, :] * ext_ref[off:off + tm, :]
        acc = term if acc is None else acc + term
    o_ref[...] = (_silu(acc[:, :tn]) * acc[:, tn:]).astype(o_ref.dtype)
    ext_ref[0:SUBLANES, :] = ext_ref[tm:tm + SUBLANES, :]


def ffn_up_call(h, w_up, conv_w, tm=1024, tn=512):
    L, D = h.shape
    nb = FFN_DIM // tn
    kern = functools.partial(_ffn_up_kernel, tm=tm, tn=tn)
    return pl.pallas_call(
        kern,
        grid=(nb, L // tm),
        in_specs=[pl.BlockSpec((tm, D), lambda n, m: (m, 0)),
                  pl.BlockSpec((D, tn), lambda n, m: (0, n)),
                  pl.BlockSpec((D, tn), lambda n, m: (0, n + nb)),
                  pl.BlockSpec((FFN_CONV, tn), lambda n, m: (0, n)),
                  pl.BlockSpec((FFN_CONV, tn), lambda n, m: (0, n + nb))],
        out_specs=pl.BlockSpec((tm, tn), lambda n, m: (m, n)),
        out_shape=jax.ShapeDtypeStruct((L, FFN_DIM), BF16),
        scratch_shapes=[pltpu.VMEM((tm + SUBLANES, 2 * tn), F32)],
        compiler_params=_cparams(("parallel", "arbitrary")),
        name="ffn_up",
    )(h, w_up, w_up, conv_w, conv_w)


def _ffn_down_kernel(a_ref, w_ref, x_ref, nw_ref, *refs, last):
    acc_ref = refs[-1]
    k = pl.program_id(1)

    @pl.when(k == 0)
    def _():
        acc_ref[...] = x_ref[...]

    acc_ref[...] += jnp.dot(a_ref[...], w_ref[...], preferred_element_type=F32)

    @pl.when(k == pl.num_programs(1) - 1)
    def _():
        xo = acc_ref[...]
        if last:
            refs[0][...] = _rms_rows(xo, nw_ref[...])
        else:
            refs[0][...] = xo
            refs[1][...] = _rms_rows(xo, nw_ref[...]).astype(refs[1].dtype)


def ffn_down_call(a, w_down, x, norm_w, last, tm=512, tk=1408):
    L, D = x.shape
    K = a.shape[1]
    row = pl.BlockSpec((tm, D), lambda m, k: (m, 0))
    if last:
        out_specs = [row]
        out_shape = [jax.ShapeDtypeStruct((L, D), F32)]
    else:
        out_specs = [row, row]
        out_shape = [jax.ShapeDtypeStruct((L, D), F32), jax.ShapeDtypeStruct((L, D), BF16)]
    kern = functools.partial(_ffn_down_kernel, last=last)
    return pl.pallas_call(
        kern,
        grid=(L // tm, K // tk),
        in_specs=[pl.BlockSpec((tm, tk), lambda m, k: (m, k)),
                  pl.BlockSpec((tk, D), lambda m, k: (k, 0)),
                  row,
                  pl.BlockSpec((1, D), lambda m, k: (0, 0))],
        out_specs=out_specs,
        out_shape=out_shape,
        scratch_shapes=[pltpu.VMEM((tm, D), F32)],
        compiler_params=_cparams(("parallel", "arbitrary")),
        name="ffn_down",
    )(a, w_down, x, norm_w.reshape(1, D))


def kernel(x, mix_norm_w, w_in, s5_log_dt, s5_a_re, s5_a_im, s5_b_re, s5_b_im, s5_c_re, s5_c_im, s5_d, s5_glu_w, dn_conv_w, dn_a_log, dn_dt_bias, dn_norm_w, dn_proj_w, w_out, ffn_norm_w, ffn_up, ffn_conv_w, ffn_down, final_norm_w):
    B, L, D = x.shape
    depth = w_in.shape[0]
    assert B == 1 and D == D_MODEL
    xs = x.reshape(L, D)
    h = rmsnorm_cast(xs, mix_norm_w[0])
    for l in range(depth):
        w_l = w_in[l]
        w_main = jnp.concatenate([w_l[:, :OFF_Z], w_l[:, OFF_ALPHA:]], axis=1).astype(BF16)
        w_ba = jnp.pad(w_l[:, OFF_Z:OFF_ALPHA], ((0, 0), (0, LANES - 2 * DN_HEADS))).astype(BF16)
        proj, ba = in_proj(h, w_main, w_ba)

        abr, abi, bbr, bbi = s5_discretise(s5_log_dt[l], s5_a_re[l], s5_a_im[l], s5_b_re[l], s5_b_im[l])
        a_r, a_i, wb, wc = s5_layout_params(abr, abi, bbr, bbi, s5_c_re[l], s5_c_im[l])
        y_s5 = s5_scan(proj, a_r, a_i, wb, wc, s5_d[l])
        y_dn = deltanet(proj, ba, dn_conv_w[l], dn_a_log[l], dn_dt_bias[l], dn_norm_w[l])

        merged = merge(y_s5, y_dn, s5_glu_w[l].astype(BF16), dn_proj_w[l].astype(BF16), proj)
        xs, h = out_proj(merged, w_out[l].astype(BF16), xs, ffn_norm_w[l])
        act = ffn_up_call(h, ffn_up[l].astype(BF16), ffn_conv_w[l])
        if l + 1 < depth:
            xs, h = ffn_down_call(act, ffn_down[l].astype(BF16), xs, mix_norm_w[l + 1], last=False)
        else:
            (out,) = ffn_down_call(act, ffn_down[l].astype(BF16), xs, final_norm_w, last=True)
    return out.reshape(B, L, D)
```

```python
import functools
import math

import jax
import jax.numpy as jnp
from jax import lax
from jax.experimental import pallas as pl
from jax.experimental.pallas import tpu as pltpu

F32 = jnp.float32
BF16 = jnp.bfloat16

D_MODEL = 2048
S5_WIDTH = 1024
S5_GROUP = 16
S5_GROUPS = 64
S5_STATE = 64
DN_HEADS = 8
DN_DK = 128
DN_DV = 128
DN_QK_WIDTH = DN_HEADS * DN_DK
DN_V_WIDTH = DN_HEADS * DN_DV
DN_CONV = 4
DN_CHUNK = 64
DN_SOLVE_BLOCK = 16
FFN_DIM = 5632
FFN_CONV = 3
NORM_EPS = 1e-6

OFF_U = S5_WIDTH
OFF_QKV = OFF_U + 2 * DN_QK_WIDTH + DN_V_WIDTH
OFF_Z = OFF_QKV + DN_V_WIDTH
OFF_BETA = OFF_Z + DN_HEADS
OFF_ALPHA = OFF_BETA + DN_HEADS
OFF_GS = OFF_ALPHA + D_MODEL
N_IN = OFF_GS + D_MODEL
N_MAIN = N_IN - 2 * DN_HEADS

LANES = 128
SUBLANES = 8
VMEM_LIMIT = 56 * 1024 * 1024

S5_SEGS = SUBLANES
S5_GB = 8
S5_NBLK = S5_GROUPS // S5_GB
S5_BLK_CH = S5_GB * S5_GROUP
S5_BLK_ST = S5_GB * S5_STATE

HI = lax.Precision.HIGHEST


def _cparams(sem):
    return pltpu.CompilerParams(dimension_semantics=sem, vmem_limit_bytes=VMEM_LIMIT)


def _sigmoid(x):
    return 1.0 / (1.0 + jnp.exp(-x))


def _silu(x):
    return x * _sigmoid(x)


def _softplus(x):
    return jnp.maximum(x, 0.0) + jnp.log1p(jnp.exp(-jnp.abs(x)))


def _rms_rows(x, w):
    ms = jnp.mean(x * x, axis=-1, keepdims=True)
    return x * lax.rsqrt(ms + NORM_EPS) * w


def _rmsnorm_kernel(x_ref, w_ref, o_ref):
    o_ref[...] = _rms_rows(x_ref[...], w_ref[...]).astype(o_ref.dtype)


def rmsnorm_cast(x, w, tm=512):
    L, D = x.shape
    return pl.pallas_call(
        _rmsnorm_kernel,
        grid=(L // tm,),
        in_specs=[pl.BlockSpec((tm, D), lambda m: (m, 0)),
                  pl.BlockSpec((1, D), lambda m: (0, 0))],
        out_specs=pl.BlockSpec((tm, D), lambda m: (m, 0)),
        out_shape=jax.ShapeDtypeStruct((L, D), BF16),
        compiler_params=_cparams(("parallel",)),
        name="rmsnorm",
    )(x, w.reshape(1, D))


def _in_proj_kernel(h_ref, w_ref, wba_ref, o_ref, ba_ref):
    h = h_ref[...]
    o_ref[...] = jnp.dot(h, w_ref[...], preferred_element_type=F32)

    @pl.when(pl.program_id(1) == 0)
    def _():
        ba_ref[...] = jnp.dot(h, wba_ref[...], preferred_element_type=F32)


def in_proj(h, w_main, w_ba, tm=1024, tn=1024):
    L, D = h.shape
    N = w_main.shape[1]
    return pl.pallas_call(
        _in_proj_kernel,
        grid=(L // tm, N // tn),
        in_specs=[pl.BlockSpec((tm, D), lambda m, n: (m, 0)),
                  pl.BlockSpec((D, tn), lambda m, n: (0, n)),
                  pl.BlockSpec((D, LANES), lambda m, n: (0, 0))],
        out_specs=[pl.BlockSpec((tm, tn), lambda m, n: (m, n)),
                   pl.BlockSpec((tm, LANES), lambda m, n: (m, 0))],
        out_shape=[jax.ShapeDtypeStruct((L, N), F32),
                   jax.ShapeDtypeStruct((L, LANES), F32)],
        compiler_params=_cparams(("parallel", "arbitrary")),
        name="in_proj",
    )(h, w_main, w_ba)


def _s5_disc_kernel(ldt_ref, are_ref, aim_ref, bre_ref, bim_ref,
                    abr_ref, abi_ref, bbr_ref, bbi_ref):
    lr = are_ref[...]
    li = aim_ref[...]
    dt = jnp.exp(ldt_ref[...])
    mag = jnp.exp(lr * dt)
    abar_re = mag * jnp.cos(li * dt)
    abar_im = mag * jnp.sin(li * dt)
    den = lr * lr + li * li
    nr = abar_re - 1.0
    ni = abar_im
    coef_re = (nr * lr + ni * li) / den
    coef_im = (ni * lr - nr * li) / den
    br = bre_ref[...]
    bi = bim_ref[...]
    abr_ref[...] = abar_re
    abi_ref[...] = abar_im
    bbr_ref[...] = coef_re * br - coef_im * bi
    bbi_ref[...] = coef_re * bi + coef_im * br


def s5_discretise(log_dt, a_re, a_im, b_re, b_im):
    G, P, HG = S5_GROUPS, S5_STATE, S5_GROUP
    bt_re = jnp.transpose(b_re, (0, 2, 1))
    bt_im = jnp.transpose(b_im, (0, 2, 1))
    return pl.pallas_call(
        _s5_disc_kernel,
        out_shape=[jax.ShapeDtypeStruct((G, 1, P), F32),
                   jax.ShapeDtypeStruct((G, 1, P), F32),
                   jax.ShapeDtypeStruct((G, HG, P), F32),
                   jax.ShapeDtypeStruct((G, HG, P), F32)],
        name="s5_disc",
    )(jnp.broadcast_to(log_dt.reshape(G, 1, 1), (G, 1, P)), a_re.reshape(G, 1, P), a_im.reshape(G, 1, P),
      bt_re, bt_im)


def _block_diag(t):
    nb, gb, r, c = t.shape
    eye = jnp.eye(gb, dtype=t.dtype)
    out = t[:, :, :, None, :] * eye[None, :, None, :, None]
    return out.reshape(nb, gb * r, gb * c)


def s5_layout_params(abar_re, abar_im, bbar_re, bbar_im, c_re, c_im):
    P, HG = S5_STATE, S5_GROUP
    a_r = jnp.broadcast_to(abar_re.reshape(1, S5_GROUPS * P), (S5_SEGS, S5_GROUPS * P))
    a_i = jnp.broadcast_to(abar_im.reshape(1, S5_GROUPS * P), (S5_SEGS, S5_GROUPS * P))
    wb = jnp.concatenate(
        [_block_diag(bbar_re.reshape(S5_NBLK, S5_GB, HG, P)),
         _block_diag(bbar_im.reshape(S5_NBLK, S5_GB, HG, P))], axis=-1).astype(BF16)
    ct_re = jnp.transpose(c_re, (0, 2, 1)).reshape(S5_NBLK, S5_GB, P, HG)
    ct_im = jnp.transpose(c_im, (0, 2, 1)).reshape(S5_NBLK, S5_GB, P, HG)
    wc = jnp.concatenate([_block_diag(ct_re), -_block_diag(ct_im)], axis=1).astype(BF16)
    return a_r, a_i, wb, wc


def _s5_kernel(u_ref, p_ref, pt_ref, wb_ref, ar_ref, ai_ref, wc_ref, d_ref, y_ref,
               bu_ref, x_ref, sr_ref, si_ref, *, ti, seg_len):
    R = S5_SEGS * ti
    NS = S5_BLK_ST
    ph = pl.program_id(0)
    it = pl.program_id(1)

    @pl.when(jnp.logical_and(ph == 0, it == 0))
    def _():
        sr_ref[...] = jnp.zeros_like(sr_ref)
        si_ref[...] = jnp.zeros_like(si_ref)

    @pl.when(jnp.logical_and(ph == 1, it == 0))
    def _():
        pr, pi = ar_ref[...], ai_ref[...]
        for _ in range(int(math.log2(seg_len))):
            pr, pi = pr * pr - pi * pi, 2.0 * pr * pi
        er, ei = sr_ref[...], si_ref[...]
        row = lax.broadcasted_iota(jnp.int32, er.shape, 0)
        zr = jnp.zeros_like(er)
        zi = jnp.zeros_like(ei)
        for _ in range(S5_SEGS - 1):
            nr = pr * zr - pi * zi + er
            ni = pr * zi + pi * zr + ei
            zr = jnp.where(row >= 1, pltpu.roll(nr, 1, 0), 0.0)
            zi = jnp.where(row >= 1, pltpu.roll(ni, 1, 0), 0.0)
        sr_ref[...] = zr
        si_ref[...] = zi

    u_nat = u_ref[...].reshape(R, S5_WIDTH)
    u2 = jnp.dot(p_ref[...], u_nat.astype(BF16), preferred_element_type=F32).astype(BF16)
    for b in range(S5_NBLK):
        bu_ref[:, b * 2 * NS:(b + 1) * 2 * NS] = jnp.dot(
            u2[:, b * S5_BLK_CH:(b + 1) * S5_BLK_CH], wb_ref[b], preferred_element_type=F32)

    def scan(store):
        for b in range(S5_NBLK):
            ar = ar_ref[:, b * NS:(b + 1) * NS]
            ai = ai_ref[:, b * NS:(b + 1) * NS]
            c_re = b * 2 * NS
            c_im = c_re + NS

            def body(i, carry):
                xr, xi = carry
                r0 = pl.multiple_of(i * SUBLANES, SUBLANES)
                br = bu_ref[pl.ds(r0, SUBLANES), c_re:c_re + NS]
                bi = bu_ref[pl.ds(r0, SUBLANES), c_im:c_im + NS]
                nr = ar * xr - ai * xi + br
                ni = ar * xi + ai * xr + bi
                if store:
                    x_ref[pl.ds(r0, SUBLANES), c_re:c_re + NS] = nr
                    x_ref[pl.ds(r0, SUBLANES), c_im:c_im + NS] = ni
                return nr, ni

            xr, xi = lax.fori_loop(
                0, ti, body, (sr_ref[:, b * NS:(b + 1) * NS], si_ref[:, b * NS:(b + 1) * NS]), unroll=4)
            sr_ref[:, b * NS:(b + 1) * NS] = xr
            si_ref[:, b * NS:(b + 1) * NS] = xi

    @pl.when(ph == 0)
    def _():
        scan(False)

    @pl.when(ph == 1)
    def _():
        scan(True)
        ys = []
        for b in range(S5_NBLK):
            xb = x_ref[:, b * 2 * NS:(b + 1) * 2 * NS].astype(BF16)
            ys.append(jnp.dot(xb, wc_ref[b], preferred_element_type=F32))
        y2 = jnp.concatenate(ys, axis=1)
        y_hi = y2.astype(BF16)
        y_lo = (y2 - y_hi.astype(F32)).astype(BF16)
        pt = pt_ref[...]
        y_nat = (jnp.dot(pt, y_hi, preferred_element_type=F32)
                 + jnp.dot(pt, y_lo, preferred_element_type=F32))
        y_nat = y_nat + d_ref[...] * u_nat
        y = 0.5 * y_nat * (1.0 + lax.erf(y_nat * (1.0 / math.sqrt(2.0))))
        y_ref[...] = y.reshape(S5_SEGS, ti, S5_WIDTH).astype(y_ref.dtype)


def s5_scan(proj, a_r, a_i, wb, wc, d, ti=32):
    L = proj.shape[0]
    seg_len = L // S5_SEGS
    assert seg_len * S5_SEGS == L and seg_len % ti == 0 and (seg_len & (seg_len - 1)) == 0
    R = S5_SEGS * ti
    nt = seg_len // ti
    proj3 = proj.reshape(S5_SEGS, seg_len, proj.shape[1])
    r = jnp.arange(R)
    perm = jnp.zeros((R, R), BF16).at[r, (r % S5_SEGS) * ti + r // S5_SEGS].set(1)
    perm_t = perm.T
    ns = S5_GROUPS * S5_STATE
    kern = functools.partial(_s5_kernel, ti=ti, seg_len=seg_len)
    y = pl.pallas_call(
        kern,
        grid=(2, nt),
        in_specs=[pl.BlockSpec((S5_SEGS, ti, S5_WIDTH), lambda p, t: (0, t, 0)),
                  pl.BlockSpec((R, R), lambda p, t: (0, 0)),
                  pl.BlockSpec((R, R), lambda p, t: (0, 0)),
                  pl.BlockSpec((S5_NBLK, S5_BLK_CH, 2 * S5_BLK_ST), lambda p, t: (0, 0, 0)),
                  pl.BlockSpec((S5_SEGS, ns), lambda p, t: (0, 0)),
                  pl.BlockSpec((S5_SEGS, ns), lambda p, t: (0, 0)),
                  pl.BlockSpec((S5_NBLK, 2 * S5_BLK_ST, S5_BLK_CH), lambda p, t: (0, 0, 0)),
                  pl.BlockSpec((1, S5_WIDTH), lambda p, t: (0, 0))],
        out_specs=pl.BlockSpec((S5_SEGS, ti, S5_WIDTH), lambda p, t: (0, t * p, 0)),
        out_shape=jax.ShapeDtypeStruct((S5_SEGS, seg_len, S5_WIDTH), BF16),
        scratch_shapes=[pltpu.VMEM((R, 2 * ns), F32),
                        pltpu.VMEM((R, 2 * ns), F32),
                        pltpu.VMEM((S5_SEGS, ns), F32),
                        pltpu.VMEM((S5_SEGS, ns), F32)],
        compiler_params=_cparams(("arbitrary", "arbitrary")),
        name="s5_scan",
    )(proj3, perm, perm_t, wb, a_r, a_i, wc, d.reshape(1, S5_WIDTH))
    return y.reshape(L, S5_WIDTH)


def _dot_nt(a, b):
    return lax.dot_general(a, b, (((1,), (1,)), ((), ())), preferred_element_type=F32)


def _dot_tn(a, b, precision=None):
    return lax.dot_general(a, b, (((0,), (0,)), ((), ())), preferred_element_type=F32,
                           precision=precision)


def _dn_kernel(q_ref, k_ref, v_ref, z_ref, ba_ref, cw_ref, alog_ref, dtb_ref, nw_ref, o_ref,
               ext_ref, act_ref, gate_ref, s_ref, *, tm):
    C = DN_CHUNK
    W = DN_QK_WIDTH
    it = pl.program_id(0)

    @pl.when(it == 0)
    def _():
        ext_ref[0:SUBLANES, :] = jnp.zeros((SUBLANES, 3 * W), F32)
        s_ref[...] = jnp.zeros_like(s_ref)

    ext_ref[SUBLANES:SUBLANES + tm, 0:W] = q_ref[...]
    ext_ref[SUBLANES:SUBLANES + tm, W:2 * W] = k_ref[...]
    ext_ref[SUBLANES:SUBLANES + tm, 2 * W:3 * W] = v_ref[...]
    cw = cw_ref[...]
    acc = None
    for j in range(DN_CONV):
        off = SUBLANES - (DN_CONV - 1) + j
        term = cw[j:j + 1, :] * ext_ref[off:off + tm, :]
        acc = term if acc is None else acc + term
    act_ref[...] = _silu(acc)
    ext_ref[0:SUBLANES, :] = ext_ref[tm:tm + SUBLANES, :]

    ba = ba_ref[...]
    lane = lax.broadcasted_iota(jnp.int32, ba.shape, 1)
    beta = _sigmoid(ba)
    g = -jnp.exp(alog_ref[...]) * _softplus(ba + dtb_ref[...])
    gate_ref[...] = jnp.where(lane < DN_HEADS, beta, g)

    ri = lax.broadcasted_iota(jnp.int32, (C, C), 0)
    ci = lax.broadcasted_iota(jnp.int32, (C, C), 1)
    tril = ri >= ci
    strict = ri > ci
    tril_f = tril.astype(F32)
    triu_f = (ri <= ci).astype(F32)
    eye_f = (ri == ci).astype(F32)
    blk = DN_SOLVE_BLOCK
    same_blk = (ri // blk) == (ci // blk)
    sub_blk = ((ri // blk) == (ci // blk) + 1) & ((ri // (2 * blk)) == (ci // (2 * blk)))
    low_half = (ri >= C // 2) & (ci < C // 2)
    nw = nw_ref[...]
    heads = range(DN_HEADS)

    def bdot(a, b):
        return jnp.dot(a.astype(BF16), b.astype(BF16), preferred_element_type=F32)

    def chunk(c, carry):
        r0 = pl.multiple_of(c * C, C)
        gt = gate_ref[pl.ds(r0, C), :]
        gc = jnp.dot(tril_f, gt, preferred_element_type=F32, precision=HI)
        gc_t = _dot_tn(gt, triu_f, precision=HI)
        q, k, v, kb, vb, gcol, glast, eg, dec = [], [], [], [], [], [], [], [], []
        for h in heads:
            lo = h * DN_DK
            qh = act_ref[pl.ds(r0, C), lo:lo + DN_DK]
            kh = act_ref[pl.ds(r0, C), W + lo:W + lo + DN_DK]
            vh = act_ref[pl.ds(r0, C), 2 * W + lo:2 * W + lo + DN_DV]
            q.append(qh * lax.rsqrt(jnp.sum(qh * qh, axis=-1, keepdims=True) + NORM_EPS) * (DN_DK ** -0.5))
            kh = kh * lax.rsqrt(jnp.sum(kh * kh, axis=-1, keepdims=True) + NORM_EPS)
            k.append(kh)
            bcol = gt[:, h:h + 1]
            gc_h = gc[:, DN_HEADS + h:DN_HEADS + h + 1]
            grow = gc_t[DN_HEADS + h:DN_HEADS + h + 1, :]
            gcol.append(gc_h)
            glast.append(gc_h[C - 1:C, :])
            eg.append(jnp.exp(gc_h))
            dec.append(jnp.where(tril, jnp.exp(jnp.where(tril, gc_h - grow, 0.0)), 0.0))
            kb.append(kh * bcol)
            vb.append(vh * bcol)
            v.append(vh)
        k16 = [k[h].astype(BF16) for h in heads]
        kk = [_dot_nt(kb[h].astype(BF16), k16[h]) for h in heads]
        qk = [_dot_nt(q[h].astype(BF16), k16[h]) for h in heads]
        lmat = [jnp.where(strict, kk[h] * dec[h], 0.0) for h in heads]
        attn = [(qk[h] * dec[h]).astype(BF16) for h in heads]

        def split(x):
            hi = x.astype(BF16)
            return hi, (x - hi.astype(F32)).astype(BF16)

        def dot3(a, b):
            (ah, al), (bh, bl) = a, b
            return (jnp.dot(ah, bh, preferred_element_type=F32)
                    + jnp.dot(al, bh, preferred_element_type=F32)
                    + jnp.dot(ah, bl, preferred_element_type=F32))

        pw = [jnp.where(same_blk, -lmat[h], 0.0) for h in heads]
        tinv = [eye_f + pw[h] for h in heads]
        pws = [split(pw[h]) for h in heads]
        for _ in range(int(math.log2(blk)) - 1):
            pw = [dot3(pws[h], pws[h]) for h in heads]
            pws = [split(pw[h]) for h in heads]
            tinv = [tinv[h] + dot3(split(tinv[h]), pws[h]) for h in heads]
        x1 = [bdot(jnp.where(sub_blk, lmat[h], 0.0), tinv[h]) for h in heads]
        tinv = [tinv[h] - bdot(tinv[h], x1[h]) for h in heads]
        x2 = [bdot(jnp.where(low_half, lmat[h], 0.0), tinv[h]) for h in heads]
        tinv = [tinv[h] - bdot(tinv[h], x2[h]) for h in heads]

        rhs = [jnp.concatenate([vb[h], kb[h] * eg[h]], axis=1) for h in heads]
        sol = [bdot(tinv[h], rhs[h]) for h in heads]
        s = [s_ref[h] for h in heads]
        s16 = [s[h].astype(BF16) for h in heads]
        wq = [jnp.concatenate([sol[h][:, DN_DV:], q[h] * eg[h]], axis=0).astype(BF16) for h in heads]
        wqs = [jnp.dot(wq[h], s16[h], preferred_element_type=F32) for h in heads]
        vn16 = [(sol[h][:, :DN_DV] - wqs[h][:C, :]).astype(BF16) for h in heads]
        kd = [(k[h] * jnp.exp(glast[h] - gcol[h])).astype(BF16) for h in heads]
        o = [wqs[h][C:, :] + jnp.dot(attn[h], vn16[h], preferred_element_type=F32) for h in heads]
        for h in heads:
            s_ref[h] = s[h] * jnp.exp(glast[h]) + _dot_tn(kd[h], vn16[h])
        for h in heads:
            lo = h * DN_DV
            zh = z_ref[pl.ds(r0, C), lo:lo + DN_DV]
            oh = o[h]
            on = oh * lax.rsqrt(jnp.mean(oh * oh, axis=-1, keepdims=True) + NORM_EPS) * nw
            o_ref[pl.ds(r0, C), lo:lo + DN_DV] = (on * _silu(zh)).astype(o_ref.dtype)
        return carry

    lax.fori_loop(0, tm // C, chunk, 0)


def deltanet(proj, ba, conv_w, a_log, dt_bias, norm_w, tm=256):
    L = proj.shape[0]
    W = DN_QK_WIDTH
    pad = jnp.zeros((1, LANES), F32)
    alog_row = pad.at[0, DN_HEADS:2 * DN_HEADS].set(a_log)
    dtb_row = pad.at[0, DN_HEADS:2 * DN_HEADS].set(dt_bias)
    kern = functools.partial(_dn_kernel, tm=tm)
    return pl.pallas_call(
        kern,
        grid=(L // tm,),
        in_specs=[pl.BlockSpec((tm, W), lambda t: (t, 1)),
                  pl.BlockSpec((tm, W), lambda t: (t, 2)),
                  pl.BlockSpec((tm, W), lambda t: (t, 3)),
                  pl.BlockSpec((tm, W), lambda t: (t, 4)),
                  pl.BlockSpec((tm, LANES), lambda t: (t, 0)),
                  pl.BlockSpec((DN_CONV, 3 * W), lambda t: (0, 0)),
                  pl.BlockSpec((1, LANES), lambda t: (0, 0)),
                  pl.BlockSpec((1, LANES), lambda t: (0, 0)),
                  pl.BlockSpec((1, DN_DV), lambda t: (0, 0))],
        out_specs=pl.BlockSpec((tm, DN_V_WIDTH), lambda t: (t, 0)),
        out_shape=jax.ShapeDtypeStruct((L, DN_V_WIDTH), BF16),
        scratch_shapes=[pltpu.VMEM((tm + SUBLANES, 3 * W), F32),
                        pltpu.VMEM((tm, 3 * W), F32),
                        pltpu.VMEM((tm, LANES), F32),
                        pltpu.VMEM((DN_HEADS, DN_DK, DN_DV), F32)],
        compiler_params=_cparams(("arbitrary",)),
        name="deltanet",
    )(proj, proj, proj, proj, ba, conv_w, alog_row, dtb_row, norm_w.reshape(1, DN_DV))


def _merge_kernel(ys_ref, yd_ref, wga_ref, wgb_ref, wdn_ref, gs_ref, gd_ref, o_ref,
                  wga16_ref, wgb16_ref, wdn16_ref):
    @pl.when(pl.program_id(1) == 0)
    def _():
        wga16_ref[...] = wga_ref[...].astype(BF16)
        wgb16_ref[...] = wgb_ref[...].astype(BF16)
        wdn16_ref[...] = wdn_ref[...].astype(BF16)

    ys = ys_ref[...]
    ga = jnp.dot(ys, wga16_ref[...], preferred_element_type=F32)
    gb = jnp.dot(ys, wgb16_ref[...], preferred_element_type=F32)
    dn = jnp.dot(yd_ref[...], wdn16_ref[...], preferred_element_type=F32)
    br_s5 = ga * _sigmoid(gb)
    o_ref[...] = (_sigmoid(gs_ref[...]) * br_s5 + _sigmoid(gd_ref[...]) * dn).astype(o_ref.dtype)


def merge(y_s5, y_dn, glu_w, dn_w, proj, layer, tm=1024, tn=512):
    L = y_s5.shape[0]
    nb = D_MODEL // tn
    gs_blk = (OFF_Z) // tn
    gd_blk = gs_blk + nb
    return pl.pallas_call(
        _merge_kernel,
        grid=(nb, L // tm),
        in_specs=[pl.BlockSpec((tm, S5_WIDTH), lambda n, m: (m, 0)),
                  pl.BlockSpec((tm, DN_V_WIDTH), lambda n, m: (m, 0)),
                  pl.BlockSpec((None, S5_WIDTH, tn), lambda n, m: (layer, 0, n)),
                  pl.BlockSpec((None, S5_WIDTH, tn), lambda n, m: (layer, 0, n + nb)),
                  pl.BlockSpec((None, DN_V_WIDTH, tn), lambda n, m: (layer, 0, n)),
                  pl.BlockSpec((tm, tn), lambda n, m: (m, n + gs_blk)),
                  pl.BlockSpec((tm, tn), lambda n, m: (m, n + gd_blk))],
        out_specs=pl.BlockSpec((tm, tn), lambda n, m: (m, n)),
        out_shape=jax.ShapeDtypeStruct((L, D_MODEL), BF16),
        scratch_shapes=[pltpu.VMEM((S5_WIDTH, tn), BF16),
                        pltpu.VMEM((S5_WIDTH, tn), BF16),
                        pltpu.VMEM((DN_V_WIDTH, tn), BF16)],
        compiler_params=_cparams(("parallel", "arbitrary")),
        name="merge",
    )(y_s5, y_dn, glu_w, glu_w, dn_w, proj, proj)


def _out_proj_kernel(m_ref, w_ref, x_ref, nw_ref, xo_ref, h_ref, w16_ref):
    @pl.when(pl.program_id(0) == 0)
    def _():
        w16_ref[...] = w_ref[...].astype(BF16)

    xo = x_ref[...] + jnp.dot(m_ref[...], w16_ref[...], preferred_element_type=F32)
    xo_ref[...] = xo
    h_ref[...] = _rms_rows(xo, nw_ref[...]).astype(h_ref.dtype)


def out_proj(merged, w_out, x, norm_w, layer, tm=512):
    L, D = x.shape
    return pl.pallas_call(
        _out_proj_kernel,
        grid=(L // tm,),
        in_specs=[pl.BlockSpec((tm, D), lambda m: (m, 0)),
                  pl.BlockSpec((None, D, D), lambda m: (layer, 0, 0), pipeline_mode=pl.Buffered(1)),
                  pl.BlockSpec((tm, D), lambda m: (m, 0)),
                  pl.BlockSpec((1, D), lambda m: (0, 0))],
        out_specs=[pl.BlockSpec((tm, D), lambda m: (m, 0)),
                   pl.BlockSpec((tm, D), lambda m: (m, 0))],
        out_shape=[jax.ShapeDtypeStruct((L, D), F32),
                   jax.ShapeDtypeStruct((L, D), BF16)],
        scratch_shapes=[pltpu.VMEM((D, D), BF16)],
        compiler_params=_cparams(("arbitrary",)),
        name="out_proj",
    )(merged, w_out, x, norm_w.reshape(1, D))


def _ffn_up_kernel(h_ref, wa_ref, wv_ref, cwa_ref, cwv_ref, o_ref, ext_ref, wa16_ref, wv16_ref, *, tm, tn):
    @pl.when(pl.program_id(1) == 0)
    def _():
        ext_ref[0:SUBLANES, :] = jnp.zeros((SUBLANES, 2 * tn), F32)
        wa16_ref[...] = wa_ref[...].astype(BF16)
        wv16_ref[...] = wv_ref[...].astype(BF16)

    h = h_ref[...]
    ext_ref[SUBLANES:SUBLANES + tm, 0:tn] = jnp.dot(h, wa16_ref[...], preferred_element_type=F32)
    ext_ref[SUBLANES:SUBLANES + tm, tn:2 * tn] = jnp.dot(h, wv16_ref[...], preferred_element_type=F32)
    cw = jnp.concatenate([cwa_ref[...], cwv_ref[...]], axis=1)
    acc = None
    for j in range(FFN_CONV):
        off = SUBLANES - (FFN_CONV - 1) + j
        term = cw[j:j + 1, :] * ext_ref[off:off + tm, :]
        acc = term if acc is None else acc + term
    o_ref[...] = (_silu(acc[:, :tn]) * acc[:, tn:]).astype(o_ref.dtype)
    ext_ref[0:SUBLANES, :] = ext_ref[tm:tm + SUBLANES, :]


def ffn_up_call(h, w_up, conv_w, layer, tm=1024, tn=512):
    L, D = h.shape
    nb = FFN_DIM // tn
    kern = functools.partial(_ffn_up_kernel, tm=tm, tn=tn)
    return pl.pallas_call(
        kern,
        grid=(nb, L // tm),
        in_specs=[pl.BlockSpec((tm, D), lambda n, m: (m, 0)),
                  pl.BlockSpec((None, D, tn), lambda n, m: (layer, 0, n)),
                  pl.BlockSpec((None, D, tn), lambda n, m: (layer, 0, n + nb)),
                  pl.BlockSpec((FFN_CONV, tn), lambda n, m: (0, n)),
                  pl.BlockSpec((FFN_CONV, tn), lambda n, m: (0, n + nb))],
        out_specs=pl.BlockSpec((tm, tn), lambda n, m: (m, n)),
        out_shape=jax.ShapeDtypeStruct((L, FFN_DIM), BF16),
        scratch_shapes=[pltpu.VMEM((tm + SUBLANES, 2 * tn), F32),
                        pltpu.VMEM((D, tn), BF16),
                        pltpu.VMEM((D, tn), BF16)],
        compiler_params=_cparams(("parallel", "arbitrary")),
        name="ffn_up",
    )(h, w_up, w_up, conv_w, conv_w)


def _ffn_down_kernel(a_ref, w_ref, x_ref, nw_ref, *refs, last):
    acc_ref = refs[-1]
    k = pl.program_id(1)

    @pl.when(k == 0)
    def _():
        acc_ref[...] = x_ref[...]

    acc_ref[...] += jnp.dot(a_ref[...], w_ref[...], preferred_element_type=F32)

    @pl.when(k == pl.num_programs(1) - 1)
    def _():
        xo = acc_ref[...]
        if last:
            refs[0][...] = _rms_rows(xo, nw_ref[...])
        else:
            refs[0][...] = xo
            refs[1][...] = _rms_rows(xo, nw_ref[...]).astype(refs[1].dtype)


def ffn_down_call(a, w_down, x, norm_w, last, tm=512, tk=1408):
    L, D = x.shape
    K = a.shape[1]
    row = pl.BlockSpec((tm, D), lambda m, k: (m, 0))
    if last:
        out_specs = [row]
        out_shape = [jax.ShapeDtypeStruct((L, D), F32)]
    else:
        out_specs = [row, row]
        out_shape = [jax.ShapeDtypeStruct((L, D), F32), jax.ShapeDtypeStruct((L, D), BF16)]
    kern = functools.partial(_ffn_down_kernel, last=last)
    return pl.pallas_call(
        kern,
        grid=(L // tm, K // tk),
        in_specs=[pl.BlockSpec((tm, tk), lambda m, k: (m, k)),
                  pl.BlockSpec((tk, D), lambda m, k: (k, 0)),
                  row,
                  pl.BlockSpec((1, D), lambda m, k: (0, 0))],
        out_specs=out_specs,
        out_shape=out_shape,
        scratch_shapes=[pltpu.VMEM((tm, D), F32)],
        compiler_params=_cparams(("parallel", "arbitrary")),
        name="ffn_down",
    )(a, w_down, x, norm_w.reshape(1, D))


def kernel(x, mix_norm_w, w_in, s5_log_dt, s5_a_re, s5_a_im, s5_b_re, s5_b_im, s5_c_re, s5_c_im, s5_d, s5_glu_w, dn_conv_w, dn_a_log, dn_dt_bias, dn_norm_w, dn_proj_w, w_out, ffn_norm_w, ffn_up, ffn_conv_w, ffn_down, final_norm_w):
    B, L, D = x.shape
    depth = w_in.shape[0]
    assert B == 1 and D == D_MODEL
    xs = x.reshape(L, D)
    h = rmsnorm_cast(xs, mix_norm_w[0])
    for l in range(depth):
        w_l = w_in[l]
        w_main = jnp.concatenate([w_l[:, :OFF_Z], w_l[:, OFF_ALPHA:]], axis=1).astype(BF16)
        w_ba = jnp.pad(w_l[:, OFF_Z:OFF_ALPHA], ((0, 0), (0, LANES - 2 * DN_HEADS))).astype(BF16)
        proj, ba = in_proj(h, w_main, w_ba)

        abr, abi, bbr, bbi = s5_discretise(s5_log_dt[l], s5_a_re[l], s5_a_im[l], s5_b_re[l], s5_b_im[l])
        a_r, a_i, wb, wc = s5_layout_params(abr, abi, bbr, bbi, s5_c_re[l], s5_c_im[l])
        y_s5 = s5_scan(proj, a_r, a_i, wb, wc, s5_d[l])
        y_dn = deltanet(proj, ba, dn_conv_w[l], dn_a_log[l], dn_dt_bias[l], dn_norm_w[l])

        merged = merge(y_s5, y_dn, s5_glu_w, dn_proj_w, proj, l)
        xs, h = out_proj(merged, w_out, xs, ffn_norm_w[l], l)
        act = ffn_up_call(h, ffn_up, ffn_conv_w[l], l)
        if l + 1 < depth:
            xs, h = ffn_down_call(act, ffn_down[l].astype(BF16), xs, mix_norm_w[l + 1], last=False)
        else:
            (out,) = ffn_down_call(act, ffn_down[l].astype(BF16), xs, final_norm_w, last=True)
    return out.reshape(B, L, D)
```

```python
import functools
import math

import jax
import jax.numpy as jnp
from jax import lax
from jax.experimental import pallas as pl
from jax.experimental.pallas import tpu as pltpu

F32 = jnp.float32
BF16 = jnp.bfloat16

D_MODEL = 2048
S5_WIDTH = 1024
S5_GROUP = 16
S5_GROUPS = 64
S5_STATE = 64
DN_HEADS = 8
DN_DK = 128
DN_DV = 128
DN_QK_WIDTH = DN_HEADS * DN_DK
DN_V_WIDTH = DN_HEADS * DN_DV
DN_CONV = 4
DN_CHUNK = 64
DN_SOLVE_BLOCK = 16
FFN_DIM = 5632
FFN_CONV = 3
NORM_EPS = 1e-6

OFF_U = S5_WIDTH
OFF_QKV = OFF_U + 2 * DN_QK_WIDTH + DN_V_WIDTH
OFF_Z = OFF_QKV + DN_V_WIDTH
OFF_BETA = OFF_Z + DN_HEADS
OFF_ALPHA = OFF_BETA + DN_HEADS
OFF_GS = OFF_ALPHA + D_MODEL
N_IN = OFF_GS + D_MODEL
N_MAIN = N_IN - 2 * DN_HEADS

LANES = 128
SUBLANES = 8
VMEM_LIMIT = 56 * 1024 * 1024

S5_SEGS = SUBLANES
S5_GB = 8
S5_NBLK = S5_GROUPS // S5_GB
S5_BLK_CH = S5_GB * S5_GROUP
S5_BLK_ST = S5_GB * S5_STATE

HI = lax.Precision.HIGHEST


def _cparams(sem):
    return pltpu.CompilerParams(dimension_semantics=sem, vmem_limit_bytes=VMEM_LIMIT)


def _sigmoid(x):
    return 1.0 / (1.0 + jnp.exp(-x))


def _silu(x):
    return x * _sigmoid(x)


def _softplus(x):
    return jnp.maximum(x, 0.0) + jnp.log1p(jnp.exp(-jnp.abs(x)))


def _rms_rows(x, w):
    ms = jnp.mean(x * x, axis=-1, keepdims=True)
    return x * lax.rsqrt(ms + NORM_EPS) * w


def _rmsnorm_kernel(x_ref, w_ref, o_ref):
    o_ref[...] = _rms_rows(x_ref[...], w_ref[...]).astype(o_ref.dtype)


def rmsnorm_cast(x, w, tm=512):
    L, D = x.shape
    return pl.pallas_call(
        _rmsnorm_kernel,
        grid=(L // tm,),
        in_specs=[pl.BlockSpec((tm, D), lambda m: (m, 0)),
                  pl.BlockSpec((1, D), lambda m: (0, 0))],
        out_specs=pl.BlockSpec((tm, D), lambda m: (m, 0)),
        out_shape=jax.ShapeDtypeStruct((L, D), BF16),
        compiler_params=_cparams(("parallel",)),
        name="rmsnorm",
    )(x, w.reshape(1, D))


def _in_proj_kernel(h_ref, w_ref, wba_ref, o_ref, ba_ref):
    h = h_ref[...]
    o_ref[...] = jnp.dot(h, w_ref[...], preferred_element_type=F32)

    @pl.when(pl.program_id(1) == 0)
    def _():
        ba_ref[...] = jnp.dot(h, wba_ref[...], preferred_element_type=F32)


def in_proj_gates(h, w_gates, w_ba, layer, tm=1024, tn=1024):
    L, D = h.shape
    N = w_gates.shape[2]
    return pl.pallas_call(
        _in_proj_kernel,
        grid=(L // tm, N // tn),
        in_specs=[pl.BlockSpec((tm, D), lambda m, n: (m, 0)),
                  pl.BlockSpec((None, D, tn), lambda m, n: (layer, 0, n)),
                  pl.BlockSpec((None, D, LANES), lambda m, n: (layer, 0, 0))],
        out_specs=[pl.BlockSpec((tm, tn), lambda m, n: (m, n)),
                   pl.BlockSpec((tm, LANES), lambda m, n: (m, 0))],
        out_shape=[jax.ShapeDtypeStruct((L, N), F32),
                   jax.ShapeDtypeStruct((L, LANES), F32)],
        compiler_params=_cparams(("parallel", "arbitrary")),
        name="in_proj_gates",
    )(h, w_gates, w_ba)


def _in_proj_mix_kernel(h_ref, w_ref, o_ref, w16_ref):
    @pl.when(pl.program_id(1) == 0)
    def _():
        w16_ref[...] = w_ref[...].astype(BF16)

    o_ref[...] = jnp.dot(h_ref[...], w16_ref[...], preferred_element_type=F32)


def in_proj_mix(h, w_in, layer, tm=1024, tn=1024):
    L, D = h.shape
    return pl.pallas_call(
        _in_proj_mix_kernel,
        grid=(OFF_Z // tn, L // tm),
        in_specs=[pl.BlockSpec((tm, D), lambda n, m: (m, 0)),
                  pl.BlockSpec((None, D, tn), lambda n, m: (layer, 0, n))],
        out_specs=pl.BlockSpec((tm, tn), lambda n, m: (m, n)),
        out_shape=jax.ShapeDtypeStruct((L, OFF_Z), F32),
        scratch_shapes=[pltpu.VMEM((D, tn), BF16)],
        compiler_params=_cparams(("parallel", "arbitrary")),
        name="in_proj_mix",
    )(h, w_in)


def _s5_disc_kernel(ldt_ref, are_ref, aim_ref, bre_ref, bim_ref,
                    abr_ref, abi_ref, bbr_ref, bbi_ref):
    lr = are_ref[...]
    li = aim_ref[...]
    dt = jnp.exp(ldt_ref[...])
    mag = jnp.exp(lr * dt)
    abar_re = mag * jnp.cos(li * dt)
    abar_im = mag * jnp.sin(li * dt)
    den = lr * lr + li * li
    nr = abar_re - 1.0
    ni = abar_im
    coef_re = (nr * lr + ni * li) / den
    coef_im = (ni * lr - nr * li) / den
    br = bre_ref[...]
    bi = bim_ref[...]
    abr_ref[...] = abar_re
    abi_ref[...] = abar_im
    bbr_ref[...] = coef_re * br - coef_im * bi
    bbi_ref[...] = coef_re * bi + coef_im * br


def s5_discretise(log_dt, a_re, a_im, b_re, b_im):
    P, HG = S5_STATE, S5_GROUP
    dg = log_dt.size
    bt_re = jnp.transpose(b_re.reshape(dg, P, HG), (0, 2, 1))
    bt_im = jnp.transpose(b_im.reshape(dg, P, HG), (0, 2, 1))
    return pl.pallas_call(
        _s5_disc_kernel,
        out_shape=[jax.ShapeDtypeStruct((dg, 1, P), F32),
                   jax.ShapeDtypeStruct((dg, 1, P), F32),
                   jax.ShapeDtypeStruct((dg, HG, P), F32),
                   jax.ShapeDtypeStruct((dg, HG, P), F32)],
        name="s5_disc",
    )(jnp.broadcast_to(log_dt.reshape(dg, 1, 1), (dg, 1, P)), a_re.reshape(dg, 1, P), a_im.reshape(dg, 1, P),
      bt_re, bt_im)


def _block_diag(t, rows_per_blk, cols_per_blk):
    tiled = jnp.tile(t, (1,) * (t.ndim - 1) + (S5_GB,))
    r = lax.broadcasted_iota(jnp.int32, tiled.shape, t.ndim - 2) // rows_per_blk
    c = lax.broadcasted_iota(jnp.int32, tiled.shape, t.ndim - 1) // cols_per_blk
    return jnp.where(r == c, tiled, jnp.zeros_like(tiled))


def s5_layout_params(abar_re, abar_im, bbar_re, bbar_im, c_re, c_im, depth):
    P, HG = S5_STATE, S5_GROUP
    ns = S5_GROUPS * P
    a_r = jnp.broadcast_to(abar_re.reshape(depth, 1, ns), (depth, S5_SEGS, ns))
    a_i = jnp.broadcast_to(abar_im.reshape(depth, 1, ns), (depth, S5_SEGS, ns))
    bb_re = bbar_re.reshape(depth, S5_NBLK, S5_BLK_CH, P)
    bb_im = bbar_im.reshape(depth, S5_NBLK, S5_BLK_CH, P)
    wb = jnp.concatenate([_block_diag(bb_re, HG, P), _block_diag(bb_im, HG, P)], axis=-1).astype(BF16)
    ct_re = jnp.transpose(c_re, (0, 1, 3, 2)).reshape(depth, S5_NBLK, S5_BLK_ST, HG)
    ct_im = jnp.transpose(c_im, (0, 1, 3, 2)).reshape(depth, S5_NBLK, S5_BLK_ST, HG)
    wc = jnp.concatenate([_block_diag(ct_re, P, HG), -_block_diag(ct_im, P, HG)], axis=-2).astype(BF16)
    return a_r, a_i, wb, wc


def _s5_kernel(u_ref, p_ref, pt_ref, wb_ref, ar_ref, ai_ref, wc_ref, d_ref, y_ref,
               bu_ref, x_ref, sr_ref, si_ref, *, ti, seg_len):
    R = S5_SEGS * ti
    NS = S5_BLK_ST
    ph = pl.program_id(0)
    it = pl.program_id(1)

    @pl.when(jnp.logical_and(ph == 0, it == 0))
    def _():
        sr_ref[...] = jnp.zeros_like(sr_ref)
        si_ref[...] = jnp.zeros_like(si_ref)

    @pl.when(jnp.logical_and(ph == 1, it == 0))
    def _():
        pr, pi = ar_ref[...], ai_ref[...]
        for _ in range(int(math.log2(seg_len))):
            pr, pi = pr * pr - pi * pi, 2.0 * pr * pi
        er, ei = sr_ref[...], si_ref[...]
        row = lax.broadcasted_iota(jnp.int32, er.shape, 0)
        zr = jnp.zeros_like(er)
        zi = jnp.zeros_like(ei)
        for _ in range(S5_SEGS - 1):
            nr = pr * zr - pi * zi + er
            ni = pr * zi + pi * zr + ei
            zr = jnp.where(row >= 1, pltpu.roll(nr, 1, 0), 0.0)
            zi = jnp.where(row >= 1, pltpu.roll(ni, 1, 0), 0.0)
        sr_ref[...] = zr
        si_ref[...] = zi

    u_nat = u_ref[...].reshape(R, S5_WIDTH)
    u2 = jnp.dot(p_ref[...], u_nat.astype(BF16), preferred_element_type=F32).astype(BF16)
    for b in range(S5_NBLK):
        bu_ref[:, b * 2 * NS:(b + 1) * 2 * NS] = jnp.dot(
            u2[:, b * S5_BLK_CH:(b + 1) * S5_BLK_CH], wb_ref[b], preferred_element_type=F32)

    def scan(store):
        for b in range(S5_NBLK):
            ar = ar_ref[:, b * NS:(b + 1) * NS]
            ai = ai_ref[:, b * NS:(b + 1) * NS]
            c_re = b * 2 * NS
            c_im = c_re + NS

            def body(i, carry):
                xr, xi = carry
                r0 = pl.multiple_of(i * SUBLANES, SUBLANES)
                br = bu_ref[pl.ds(r0, SUBLANES), c_re:c_re + NS]
                bi = bu_ref[pl.ds(r0, SUBLANES), c_im:c_im + NS]
                nr = ar * xr - ai * xi + br
                ni = ar * xi + ai * xr + bi
                if store:
                    x_ref[pl.ds(r0, SUBLANES), c_re:c_re + NS] = nr
                    x_ref[pl.ds(r0, SUBLANES), c_im:c_im + NS] = ni
                return nr, ni

            xr, xi = lax.fori_loop(
                0, ti, body, (sr_ref[:, b * NS:(b + 1) * NS], si_ref[:, b * NS:(b + 1) * NS]), unroll=4)
            sr_ref[:, b * NS:(b + 1) * NS] = xr
            si_ref[:, b * NS:(b + 1) * NS] = xi

    @pl.when(ph == 0)
    def _():
        scan(False)

    @pl.when(ph == 1)
    def _():
        scan(True)
        ys = []
        for b in range(S5_NBLK):
            xb = x_ref[:, b * 2 * NS:(b + 1) * 2 * NS].astype(BF16)
            ys.append(jnp.dot(xb, wc_ref[b], preferred_element_type=F32))
        y2 = jnp.concatenate(ys, axis=1)
        y_hi = y2.astype(BF16)
        y_lo = (y2 - y_hi.astype(F32)).astype(BF16)
        pt = pt_ref[...]
        y_nat = (jnp.dot(pt, y_hi, preferred_element_type=F32)
                 + jnp.dot(pt, y_lo, preferred_element_type=F32))
        y_nat = y_nat + d_ref[...] * u_nat
        y = 0.5 * y_nat * (1.0 + lax.erf(y_nat * (1.0 / math.sqrt(2.0))))
        y_ref[...] = y.reshape(S5_SEGS, ti, S5_WIDTH).astype(y_ref.dtype)


def s5_perm(ti):
    R = S5_SEGS * ti
    r = lax.broadcasted_iota(jnp.int32, (R, R), 0)
    c = lax.broadcasted_iota(jnp.int32, (R, R), 1)
    perm = (c == (r % S5_SEGS) * ti + r // S5_SEGS).astype(BF16)
    return perm, perm.T


def s5_scan(proj, perm, perm_t, a_r, a_i, wb, wc, d, layer, ti=32):
    L = proj.shape[0]
    seg_len = L // S5_SEGS
    assert seg_len * S5_SEGS == L and seg_len % ti == 0 and (seg_len & (seg_len - 1)) == 0
    R = S5_SEGS * ti
    nt = seg_len // ti
    proj3 = proj.reshape(S5_SEGS, seg_len, proj.shape[1])
    ns = S5_GROUPS * S5_STATE
    kern = functools.partial(_s5_kernel, ti=ti, seg_len=seg_len)
    y = pl.pallas_call(
        kern,
        grid=(2, nt),
        in_specs=[pl.BlockSpec((S5_SEGS, ti, S5_WIDTH), lambda p, t: (0, t, 0)),
                  pl.BlockSpec((R, R), lambda p, t: (0, 0)),
                  pl.BlockSpec((R, R), lambda p, t: (0, 0)),
                  pl.BlockSpec((None, S5_NBLK, S5_BLK_CH, 2 * S5_BLK_ST), lambda p, t: (layer, 0, 0, 0)),
                  pl.BlockSpec((None, S5_SEGS, ns), lambda p, t: (layer, 0, 0)),
                  pl.BlockSpec((None, S5_SEGS, ns), lambda p, t: (layer, 0, 0)),
                  pl.BlockSpec((None, S5_NBLK, 2 * S5_BLK_ST, S5_BLK_CH), lambda p, t: (layer, 0, 0, 0)),
                  pl.BlockSpec((None, 1, S5_WIDTH), lambda p, t: (layer, 0, 0))],
        out_specs=pl.BlockSpec((S5_SEGS, ti, S5_WIDTH), lambda p, t: (0, t * p, 0)),
        out_shape=jax.ShapeDtypeStruct((S5_SEGS, seg_len, S5_WIDTH), BF16),
        scratch_shapes=[pltpu.VMEM((R, 2 * ns), F32),
                        pltpu.VMEM((R, 2 * ns), F32),
                        pltpu.VMEM((S5_SEGS, ns), F32),
                        pltpu.VMEM((S5_SEGS, ns), F32)],
        compiler_params=_cparams(("arbitrary", "arbitrary")),
        name="s5_scan",
    )(proj3, perm, perm_t, wb, a_r, a_i, wc, d.reshape(-1, 1, S5_WIDTH))
    return y.reshape(L, S5_WIDTH)


def _dot_nt(a, b):
    return lax.dot_general(a, b, (((1,), (1,)), ((), ())), preferred_element_type=F32)


def _dot_tn(a, b, precision=None):
    return lax.dot_general(a, b, (((0,), (0,)), ((), ())), preferred_element_type=F32,
                           precision=precision)


def _dn_kernel(q_ref, k_ref, v_ref, z_ref, ba_ref, cw_ref, alog_ref, dtb_ref, nw_ref, o_ref,
               ext_ref, act_ref, gate_ref, s_ref, *, tm):
    C = DN_CHUNK
    W = DN_QK_WIDTH
    it = pl.program_id(0)

    @pl.when(it == 0)
    def _():
        ext_ref[0:SUBLANES, :] = jnp.zeros((SUBLANES, 3 * W), F32)
        s_ref[...] = jnp.zeros_like(s_ref)

    ext_ref[SUBLANES:SUBLANES + tm, 0:W] = q_ref[...]
    ext_ref[SUBLANES:SUBLANES + tm, W:2 * W] = k_ref[...]
    ext_ref[SUBLANES:SUBLANES + tm, 2 * W:3 * W] = v_ref[...]
    cw = cw_ref[...]
    for cb in range(3 * W // LANES):
        cols = slice(cb * LANES, (cb + 1) * LANES)
        acc = None
        for j in range(DN_CONV):
            off = SUBLANES - (DN_CONV - 1) + j
            term = cw[j:j + 1, cols] * ext_ref[off:off + tm, cols]
            acc = term if acc is None else acc + term
        act_ref[:, cols] = _silu(acc)
    ext_ref[0:SUBLANES, :] = ext_ref[tm:tm + SUBLANES, :]

    ba = ba_ref[...]
    lane = lax.broadcasted_iota(jnp.int32, ba.shape, 1)
    beta = _sigmoid(ba)
    g = -jnp.exp(alog_ref[...]) * _softplus(ba + dtb_ref[...])
    gate_ref[...] = jnp.where(lane < DN_HEADS, beta, g)

    ri = lax.broadcasted_iota(jnp.int32, (C, C), 0)
    ci = lax.broadcasted_iota(jnp.int32, (C, C), 1)
    tril = ri >= ci
    strict = ri > ci
    tril_f = tril.astype(F32)
    triu_f = (ri <= ci).astype(F32)
    eye_f = (ri == ci).astype(F32)
    blk = DN_SOLVE_BLOCK
    same_blk = (ri // blk) == (ci // blk)
    sub_blk = ((ri // blk) == (ci // blk) + 1) & ((ri // (2 * blk)) == (ci // (2 * blk)))
    low_half = (ri >= C // 2) & (ci < C // 2)
    nw = nw_ref[...]
    heads = range(DN_HEADS)

    def bdot(a, b):
        return jnp.dot(a.astype(BF16), b.astype(BF16), preferred_element_type=F32)

    def chunk(c, carry):
        r0 = pl.multiple_of(c * C, C)
        gt = gate_ref[pl.ds(r0, C), :]
        gc = jnp.dot(tril_f, gt, preferred_element_type=F32, precision=HI)
        gc_t = _dot_tn(gt, triu_f, precision=HI)
        q, k, v, kb, vb, gcol, glast, eg, dec = [], [], [], [], [], [], [], [], []
        for h in heads:
            lo = h * DN_DK
            qh = act_ref[pl.ds(r0, C), lo:lo + DN_DK]
            kh = act_ref[pl.ds(r0, C), W + lo:W + lo + DN_DK]
            vh = act_ref[pl.ds(r0, C), 2 * W + lo:2 * W + lo + DN_DV]
            q.append(qh * lax.rsqrt(jnp.sum(qh * qh, axis=-1, keepdims=True) + NORM_EPS) * (DN_DK ** -0.5))
            kh = kh * lax.rsqrt(jnp.sum(kh * kh, axis=-1, keepdims=True) + NORM_EPS)
            k.append(kh)
            bcol = gt[:, h:h + 1]
            gc_h = gc[:, DN_HEADS + h:DN_HEADS + h + 1]
            grow = gc_t[DN_HEADS + h:DN_HEADS + h + 1, :]
            gcol.append(gc_h)
            glast.append(gc_h[C - 1:C, :])
            eg.append(jnp.exp(gc_h))
            dec.append(jnp.where(tril, jnp.exp(jnp.where(tril, gc_h - grow, 0.0)), 0.0))
            kb.append(kh * bcol)
            vb.append(vh * bcol)
            v.append(vh)
        k16 = [k[h].astype(BF16) for h in heads]
        kk = [_dot_nt(kb[h].astype(BF16), k16[h]) for h in heads]
        qk = [_dot_nt(q[h].astype(BF16), k16[h]) for h in heads]
        lmat = [jnp.where(strict, kk[h] * dec[h], 0.0) for h in heads]
        attn = [(qk[h] * dec[h]).astype(BF16) for h in heads]

        def split(x):
            hi = x.astype(BF16)
            return hi, (x - hi.astype(F32)).astype(BF16)

        def dot3(a, b):
            (ah, al), (bh, bl) = a, b
            return (jnp.dot(ah, bh, preferred_element_type=F32)
                    + jnp.dot(al, bh, preferred_element_type=F32)
                    + jnp.dot(ah, bl, preferred_element_type=F32))

        pw = [jnp.where(same_blk, -lmat[h], 0.0) for h in heads]
        tinv = [eye_f + pw[h] for h in heads]
        pws = [split(pw[h]) for h in heads]
        for _ in range(int(math.log2(blk)) - 1):
            pw = [dot3(pws[h], pws[h]) for h in heads]
            pws = [split(pw[h]) for h in heads]
            tinv = [tinv[h] + dot3(split(tinv[h]), pws[h]) for h in heads]
        x1 = [bdot(jnp.where(sub_blk, lmat[h], 0.0), tinv[h]) for h in heads]
        tinv = [tinv[h] - bdot(tinv[h], x1[h]) for h in heads]
        x2 = [bdot(jnp.where(low_half, lmat[h], 0.0), tinv[h]) for h in heads]
        tinv = [tinv[h] - bdot(tinv[h], x2[h]) for h in heads]

        rhs = [jnp.concatenate([vb[h], kb[h] * eg[h]], axis=1) for h in heads]
        sol = [bdot(tinv[h], rhs[h]) for h in heads]
        s = [s_ref[h] for h in heads]
        s16 = [s[h].astype(BF16) for h in heads]
        wq = [jnp.concatenate([sol[h][:, DN_DV:], q[h] * eg[h]], axis=0).astype(BF16) for h in heads]
        wqs = [jnp.dot(wq[h], s16[h], preferred_element_type=F32) for h in heads]
        vn16 = [(sol[h][:, :DN_DV] - wqs[h][:C, :]).astype(BF16) for h in heads]
        kd = [(k[h] * jnp.exp(glast[h] - gcol[h])).astype(BF16) for h in heads]
        o = [wqs[h][C:, :] + jnp.dot(attn[h], vn16[h], preferred_element_type=F32) for h in heads]
        for h in heads:
            s_ref[h] = s[h] * jnp.exp(glast[h]) + _dot_tn(kd[h], vn16[h])
        for h in heads:
            lo = h * DN_DV
            zh = z_ref[pl.ds(r0, C), lo:lo + DN_DV]
            oh = o[h]
            on = oh * lax.rsqrt(jnp.mean(oh * oh, axis=-1, keepdims=True) + NORM_EPS) * nw
            o_ref[pl.ds(r0, C), lo:lo + DN_DV] = (on * _silu(zh)).astype(o_ref.dtype)
        return carry

    lax.fori_loop(0, tm // C, chunk, 0, unroll=2)


def deltanet(proj, ba, conv_w, a_log, dt_bias, norm_w, tm=256):
    L = proj.shape[0]
    W = DN_QK_WIDTH
    pad = jnp.zeros((1, LANES), F32)
    alog_row = pad.at[0, DN_HEADS:2 * DN_HEADS].set(a_log)
    dtb_row = pad.at[0, DN_HEADS:2 * DN_HEADS].set(dt_bias)
    kern = functools.partial(_dn_kernel, tm=tm)
    return pl.pallas_call(
        kern,
        grid=(L // tm,),
        in_specs=[pl.BlockSpec((tm, W), lambda t: (t, 1)),
                  pl.BlockSpec((tm, W), lambda t: (t, 2)),
                  pl.BlockSpec((tm, W), lambda t: (t, 3)),
                  pl.BlockSpec((tm, W), lambda t: (t, 4)),
                  pl.BlockSpec((tm, LANES), lambda t: (t, 0)),
                  pl.BlockSpec((DN_CONV, 3 * W), lambda t: (0, 0)),
                  pl.BlockSpec((1, LANES), lambda t: (0, 0)),
                  pl.BlockSpec((1, LANES), lambda t: (0, 0)),
                  pl.BlockSpec((1, DN_DV), lambda t: (0, 0))],
        out_specs=pl.BlockSpec((tm, DN_V_WIDTH), lambda t: (t, 0)),
        out_shape=jax.ShapeDtypeStruct((L, DN_V_WIDTH), BF16),
        scratch_shapes=[pltpu.VMEM((tm + SUBLANES, 3 * W), F32),
                        pltpu.VMEM((tm, 3 * W), F32),
                        pltpu.VMEM((tm, LANES), F32),
                        pltpu.VMEM((DN_HEADS, DN_DK, DN_DV), F32)],
        compiler_params=_cparams(("arbitrary",)),
        name="deltanet",
    )(proj, proj, proj, proj, ba, conv_w, alog_row, dtb_row, norm_w.reshape(1, DN_DV))


def _merge_kernel(ys_ref, yd_ref, wga_ref, wgb_ref, wdn_ref, gs_ref, gd_ref, o_ref,
                  wga16_ref, wgb16_ref, wdn16_ref):
    @pl.when(pl.program_id(1) == 0)
    def _():
        wga16_ref[...] = wga_ref[...].astype(BF16)
        wgb16_ref[...] = wgb_ref[...].astype(BF16)
        wdn16_ref[...] = wdn_ref[...].astype(BF16)

    ys = ys_ref[...]
    ga = jnp.dot(ys, wga16_ref[...], preferred_element_type=F32)
    gb = jnp.dot(ys, wgb16_ref[...], preferred_element_type=F32)
    dn = jnp.dot(yd_ref[...], wdn16_ref[...], preferred_element_type=F32)
    br_s5 = ga * _sigmoid(gb)
    o_ref[...] = (_sigmoid(gs_ref[...]) * br_s5 + _sigmoid(gd_ref[...]) * dn).astype(o_ref.dtype)


def merge(y_s5, y_dn, glu_w, dn_w, gates, layer, tm=1024, tn=512):
    L = y_s5.shape[0]
    nb = D_MODEL // tn
    gs_blk = 0
    gd_blk = nb
    return pl.pallas_call(
        _merge_kernel,
        grid=(nb, L // tm),
        in_specs=[pl.BlockSpec((tm, S5_WIDTH), lambda n, m: (m, 0)),
                  pl.BlockSpec((tm, DN_V_WIDTH), lambda n, m: (m, 0)),
                  pl.BlockSpec((None, S5_WIDTH, tn), lambda n, m: (layer, 0, n)),
                  pl.BlockSpec((None, S5_WIDTH, tn), lambda n, m: (layer, 0, n + nb)),
                  pl.BlockSpec((None, DN_V_WIDTH, tn), lambda n, m: (layer, 0, n)),
                  pl.BlockSpec((tm, tn), lambda n, m: (m, n + gs_blk)),
                  pl.BlockSpec((tm, tn), lambda n, m: (m, n + gd_blk))],
        out_specs=pl.BlockSpec((tm, tn), lambda n, m: (m, n)),
        out_shape=jax.ShapeDtypeStruct((L, D_MODEL), BF16),
        scratch_shapes=[pltpu.VMEM((S5_WIDTH, tn), BF16),
                        pltpu.VMEM((S5_WIDTH, tn), BF16),
                        pltpu.VMEM((DN_V_WIDTH, tn), BF16)],
        compiler_params=_cparams(("parallel", "arbitrary")),
        name="merge",
    )(y_s5, y_dn, glu_w, glu_w, dn_w, gates, gates)


def _out_proj_kernel(m_ref, w_ref, x_ref, nw_ref, xo_ref, h_ref, w16_ref):
    @pl.when(pl.program_id(0) == 0)
    def _():
        w16_ref[...] = w_ref[...].astype(BF16)

    xo = x_ref[...] + jnp.dot(m_ref[...], w16_ref[...], preferred_element_type=F32)
    xo_ref[...] = xo
    h_ref[...] = _rms_rows(xo, nw_ref[...]).astype(h_ref.dtype)


def out_proj(merged, w_out, x, norm_w, layer, tm=512):
    L, D = x.shape
    return pl.pallas_call(
        _out_proj_kernel,
        grid=(L // tm,),
        in_specs=[pl.BlockSpec((tm, D), lambda m: (m, 0)),
                  pl.BlockSpec((None, D, D), lambda m: (layer, 0, 0), pipeline_mode=pl.Buffered(1)),
                  pl.BlockSpec((tm, D), lambda m: (m, 0)),
                  pl.BlockSpec((1, D), lambda m: (0, 0))],
        out_specs=[pl.BlockSpec((tm, D), lambda m: (m, 0)),
                   pl.BlockSpec((tm, D), lambda m: (m, 0))],
        out_shape=[jax.ShapeDtypeStruct((L, D), F32),
                   jax.ShapeDtypeStruct((L, D), BF16)],
        scratch_shapes=[pltpu.VMEM((D, D), BF16)],
        compiler_params=_cparams(("arbitrary",)),
        name="out_proj",
    )(merged, w_out, x, norm_w.reshape(1, D))


def _ffn_up_kernel(h_ref, wa_ref, wv_ref, cwa_ref, cwv_ref, o_ref, ext_ref, w16_ref, *, tm, tn, slab):
    nslab = tn // slab

    @pl.when(pl.program_id(1) == 0)
    def _():
        ext_ref[0:SUBLANES, :] = jnp.zeros((SUBLANES, 2 * tn), F32)
        for s in range(nslab):
            w16_ref[:, 2 * s * slab:(2 * s + 1) * slab] = wa_ref[:, s * slab:(s + 1) * slab].astype(BF16)
            w16_ref[:, (2 * s + 1) * slab:(2 * s + 2) * slab] = wv_ref[:, s * slab:(s + 1) * slab].astype(BF16)

    h = h_ref[...]
    cwa = cwa_ref[...]
    cwv = cwv_ref[...]
    for s in range(nslab):
        c0 = 2 * s * slab
        ext_ref[SUBLANES:SUBLANES + tm, c0:c0 + 2 * slab] = jnp.dot(
            h, w16_ref[:, c0:c0 + 2 * slab], preferred_element_type=F32)
        cw = jnp.concatenate([cwa[:, s * slab:(s + 1) * slab], cwv[:, s * slab:(s + 1) * slab]], axis=1)
        acc = None
        for j in range(FFN_CONV):
            off = SUBLANES - (FFN_CONV - 1) + j
            term = cw[j:j + 1, :] * ext_ref[off:off + tm, c0:c0 + 2 * slab]
            acc = term if acc is None else acc + term
        o_ref[:, s * slab:(s + 1) * slab] = (_silu(acc[:, :slab]) * acc[:, slab:]).astype(o_ref.dtype)
        ext_ref[0:SUBLANES, c0:c0 + 2 * slab] = ext_ref[tm:tm + SUBLANES, c0:c0 + 2 * slab]


def ffn_up_call(h, w_up, conv_w, layer, tm=1024, tn=512, slab=512):
    L, D = h.shape
    nb = FFN_DIM // tn
    kern = functools.partial(_ffn_up_kernel, tm=tm, tn=tn, slab=slab)
    return pl.pallas_call(
        kern,
        grid=(nb, L // tm),
        in_specs=[pl.BlockSpec((tm, D), lambda n, m: (m, 0)),
                  pl.BlockSpec((None, D, tn), lambda n, m: (layer, 0, n)),
                  pl.BlockSpec((None, D, tn), lambda n, m: (layer, 0, n + nb)),
                  pl.BlockSpec((FFN_CONV, tn), lambda n, m: (0, n)),
                  pl.BlockSpec((FFN_CONV, tn), lambda n, m: (0, n + nb))],
        out_specs=pl.BlockSpec((tm, tn), lambda n, m: (m, n)),
        out_shape=jax.ShapeDtypeStruct((L, FFN_DIM), BF16),
        scratch_shapes=[pltpu.VMEM((tm + SUBLANES, 2 * tn), F32),
                        pltpu.VMEM((D, 2 * tn), BF16)],
        compiler_params=_cparams(("parallel", "arbitrary")),
        name="ffn_up",
    )(h, w_up, w_up, conv_w, conv_w)


def _ffn_down_kernel(a_ref, w_ref, x_ref, nw_ref, *refs, last):
    acc_ref = refs[0]
    k = pl.program_id(1)

    @pl.when(k == 0)
    def _():
        acc_ref[...] = x_ref[...]

    acc_ref[...] += jnp.dot(a_ref[...], w_ref[...].astype(BF16), preferred_element_type=F32)

    @pl.when(k == pl.num_programs(1) - 1)
    def _():
        xo = acc_ref[...]
        if last:
            acc_ref[...] = _rms_rows(xo, nw_ref[...])
        else:
            refs[1][...] = _rms_rows(xo, nw_ref[...]).astype(refs[1].dtype)


def ffn_down_call(a, w_down, x, norm_w, layer, last, tm=1024, tk=512):
    L, D = x.shape
    K = a.shape[1]
    row = pl.BlockSpec((tm, D), lambda m, k: (m, 0))
    if last:
        out_specs = [row]
        out_shape = [jax.ShapeDtypeStruct((L, D), F32)]
    else:
        out_specs = [row, row]
        out_shape = [jax.ShapeDtypeStruct((L, D), F32), jax.ShapeDtypeStruct((L, D), BF16)]
    kern = functools.partial(_ffn_down_kernel, last=last)
    return pl.pallas_call(
        kern,
        grid=(L // tm, K // tk),
        in_specs=[pl.BlockSpec((tm, tk), lambda m, k: (m, k)),
                  pl.BlockSpec((None, tk, D), lambda m, k: (layer, k, 0)),
                  row,
                  pl.BlockSpec((1, D), lambda m, k: (0, 0))],
        out_specs=out_specs,
        out_shape=out_shape,
        compiler_params=_cparams(("parallel", "arbitrary")),
        name="ffn_down",
    )(a, w_down, x, norm_w.reshape(1, D))


def kernel(x, mix_norm_w, w_in, s5_log_dt, s5_a_re, s5_a_im, s5_b_re, s5_b_im, s5_c_re, s5_c_im, s5_d, s5_glu_w, dn_conv_w, dn_a_log, dn_dt_bias, dn_norm_w, dn_proj_w, w_out, ffn_norm_w, ffn_up, ffn_conv_w, ffn_down, final_norm_w):
    B, L, D = x.shape
    depth = w_in.shape[0]
    assert B == 1 and D == D_MODEL
    xs = x.reshape(L, D)
    h = rmsnorm_cast(xs, mix_norm_w[0])
    abr, abi, bbr, bbi = s5_discretise(s5_log_dt, s5_a_re, s5_a_im, s5_b_re, s5_b_im)
    s5_ar, s5_ai, s5_wb, s5_wc = s5_layout_params(abr, abi, bbr, bbi, s5_c_re, s5_c_im, depth)
    s5_ti = 32
    perm, perm_t = s5_perm(s5_ti)
    w_gates = w_in[:, :, OFF_ALPHA:].astype(BF16)
    w_ba = jnp.pad(w_in[:, :, OFF_Z:OFF_ALPHA], ((0, 0), (0, 0), (0, LANES - 2 * DN_HEADS))).astype(BF16)
    for l in range(depth):
        proj = in_proj_mix(h, w_in, l)
        gates, ba = in_proj_gates(h, w_gates, w_ba, l)

        y_s5 = s5_scan(proj, perm, perm_t, s5_ar, s5_ai, s5_wb, s5_wc, s5_d, l, ti=s5_ti)
        y_dn = deltanet(proj, ba, dn_conv_w[l], dn_a_log[l], dn_dt_bias[l], dn_norm_w[l])

        merged = merge(y_s5, y_dn, s5_glu_w, dn_proj_w, gates, l)
        xs, h = out_proj(merged, w_out, xs, ffn_norm_w[l], l)
        act = ffn_up_call(h, ffn_up, ffn_conv_w[l], l)
        if l + 1 < depth:
            xs, h = ffn_down_call(act, ffn_down, xs, mix_norm_w[l + 1], l, last=False)
        else:
            (out,) = ffn_down_call(act, ffn_down, xs, final_norm_w, l, last=True)
    return out.reshape(B, L, D)
```

```python
import functools
import math

import jax
import jax.numpy as jnp
from jax import lax
from jax.experimental import pallas as pl
from jax.experimental.pallas import tpu as pltpu

F32 = jnp.float32
BF16 = jnp.bfloat16

D_MODEL = 2048
S5_WIDTH = 1024
S5_GROUP = 16
S5_GROUPS = 64
S5_STATE = 64
DN_HEADS = 8
DN_DK = 128
DN_DV = 128
DN_QK_WIDTH = DN_HEADS * DN_DK
DN_V_WIDTH = DN_HEADS * DN_DV
DN_CONV = 4
DN_CHUNK = 64
DN_SOLVE_BLOCK = 16
FFN_DIM = 5632
FFN_CONV = 3
NORM_EPS = 1e-6

OFF_U = S5_WIDTH
OFF_QKV = OFF_U + 2 * DN_QK_WIDTH + DN_V_WIDTH
OFF_Z = OFF_QKV + DN_V_WIDTH
OFF_BETA = OFF_Z + DN_HEADS
OFF_ALPHA = OFF_BETA + DN_HEADS
OFF_GS = OFF_ALPHA + D_MODEL
N_IN = OFF_GS + D_MODEL
N_MAIN = N_IN - 2 * DN_HEADS

LANES = 128
SUBLANES = 8
VMEM_LIMIT = 56 * 1024 * 1024

S5_SEGS = SUBLANES
S5_GB = 8
S5_NBLK = S5_GROUPS // S5_GB
S5_BLK_CH = S5_GB * S5_GROUP
S5_BLK_ST = S5_GB * S5_STATE

HI = lax.Precision.HIGHEST


def _cparams(sem):
    return pltpu.CompilerParams(dimension_semantics=sem, vmem_limit_bytes=VMEM_LIMIT)


def _sigmoid(x):
    return 1.0 / (1.0 + jnp.exp(-x))


def _silu(x):
    hx = 0.5 * x
    return hx + hx * jnp.tanh(hx)


def _softplus(x):
    return jnp.maximum(x, 0.0) + jnp.log1p(jnp.exp(-jnp.abs(x)))


def _rms_rows(x, w):
    ms = jnp.mean(x * x, axis=-1, keepdims=True)
    return x * lax.rsqrt(ms + NORM_EPS) * w


def _rmsnorm_kernel(x_ref, w_ref, o_ref):
    o_ref[...] = _rms_rows(x_ref[...], w_ref[...]).astype(o_ref.dtype)


def rmsnorm_cast(x, w, tm=512):
    L, D = x.shape
    return pl.pallas_call(
        _rmsnorm_kernel,
        grid=(L // tm,),
        in_specs=[pl.BlockSpec((tm, D), lambda m: (m, 0)),
                  pl.BlockSpec((1, D), lambda m: (0, 0))],
        out_specs=pl.BlockSpec((tm, D), lambda m: (m, 0)),
        out_shape=jax.ShapeDtypeStruct((L, D), BF16),
        compiler_params=_cparams(("parallel",)),
        name="rmsnorm",
    )(x, w.reshape(1, D))


def _in_proj_kernel(h_ref, w_ref, wba_ref, o_ref, ba_ref):
    h = h_ref[...]
    o_ref[...] = jnp.dot(h, w_ref[...], preferred_element_type=F32)

    @pl.when(pl.program_id(1) == 0)
    def _():
        ba_ref[...] = jnp.dot(h, wba_ref[...], preferred_element_type=F32)


def in_proj_gates(h, w_gates, w_ba, layer, tm=1024, tn=1024):
    L, D = h.shape
    N = w_gates.shape[2]
    return pl.pallas_call(
        _in_proj_kernel,
        grid=(L // tm, N // tn),
        in_specs=[pl.BlockSpec((tm, D), lambda m, n: (m, 0)),
                  pl.BlockSpec((None, D, tn), lambda m, n: (layer, 0, n)),
                  pl.BlockSpec((None, D, LANES), lambda m, n: (layer, 0, 0))],
        out_specs=[pl.BlockSpec((tm, tn), lambda m, n: (m, n)),
                   pl.BlockSpec((tm, LANES), lambda m, n: (m, 0))],
        out_shape=[jax.ShapeDtypeStruct((L, N), F32),
                   jax.ShapeDtypeStruct((L, LANES), F32)],
        compiler_params=_cparams(("parallel", "arbitrary")),
        name="in_proj_gates",
    )(h, w_gates, w_ba)


def _in_proj_mix_kernel(h_ref, w_ref, o_ref):
    o_ref[...] = jnp.dot(h_ref[...], w_ref[...], preferred_element_type=F32)


def in_proj_mix(h, w_mix, layer, tm=1024, tn=1024):
    L, D = h.shape
    N = w_mix.shape[2]
    return pl.pallas_call(
        _in_proj_mix_kernel,
        grid=(N // tn, L // tm),
        in_specs=[pl.BlockSpec((tm, D), lambda n, m: (m, 0)),
                  pl.BlockSpec((None, D, tn), lambda n, m: (layer, 0, n))],
        out_specs=pl.BlockSpec((tm, tn), lambda n, m: (m, n)),
        out_shape=jax.ShapeDtypeStruct((L, N), F32),
        compiler_params=_cparams(("parallel", "arbitrary")),
        name="in_proj_mix",
    )(h, w_mix)


def _s5_disc_kernel(ldt_ref, are_ref, aim_ref, bre_ref, bim_ref,
                    abr_ref, abi_ref, bbr_ref, bbi_ref):
    lr = are_ref[...]
    li = aim_ref[...]
    dt = jnp.exp(ldt_ref[...])
    mag = jnp.exp(lr * dt)
    abar_re = mag * jnp.cos(li * dt)
    abar_im = mag * jnp.sin(li * dt)
    den = lr * lr + li * li
    nr = abar_re - 1.0
    ni = abar_im
    coef_re = (nr * lr + ni * li) / den
    coef_im = (ni * lr - nr * li) / den
    br = bre_ref[...]
    bi = bim_ref[...]
    abr_ref[...] = abar_re
    abi_ref[...] = abar_im
    bbr_ref[...] = coef_re * br - coef_im * bi
    bbi_ref[...] = coef_re * bi + coef_im * br


def s5_discretise(log_dt, a_re, a_im, b_re, b_im):
    P, HG = S5_STATE, S5_GROUP
    dg = log_dt.size
    bt_re = jnp.transpose(b_re.reshape(dg, P, HG), (0, 2, 1))
    bt_im = jnp.transpose(b_im.reshape(dg, P, HG), (0, 2, 1))
    return pl.pallas_call(
        _s5_disc_kernel,
        out_shape=[jax.ShapeDtypeStruct((dg, 1, P), F32),
                   jax.ShapeDtypeStruct((dg, 1, P), F32),
                   jax.ShapeDtypeStruct((dg, HG, P), F32),
                   jax.ShapeDtypeStruct((dg, HG, P), F32)],
        name="s5_disc",
    )(jnp.broadcast_to(log_dt.reshape(dg, 1, 1), (dg, 1, P)), a_re.reshape(dg, 1, P), a_im.reshape(dg, 1, P),
      bt_re, bt_im)


def _block_diag(t, rows_per_blk, cols_per_blk):
    tiled = jnp.tile(t, (1,) * (t.ndim - 1) + (S5_GB,))
    r = lax.broadcasted_iota(jnp.int32, tiled.shape, t.ndim - 2) // rows_per_blk
    c = lax.broadcasted_iota(jnp.int32, tiled.shape, t.ndim - 1) // cols_per_blk
    return jnp.where(r == c, tiled, jnp.zeros_like(tiled))


def s5_layout_params(abar_re, abar_im, bbar_re, bbar_im, c_re, c_im, depth):
    P, HG = S5_STATE, S5_GROUP
    ns = S5_GROUPS * P
    a_r = jnp.broadcast_to(abar_re.reshape(depth, 1, ns), (depth, S5_SEGS, ns))
    a_i = jnp.broadcast_to(abar_im.reshape(depth, 1, ns), (depth, S5_SEGS, ns))
    bb_re = bbar_re.reshape(depth, S5_NBLK, S5_BLK_CH, P)
    bb_im = bbar_im.reshape(depth, S5_NBLK, S5_BLK_CH, P)
    wb = jnp.concatenate([_block_diag(bb_re, HG, P), _block_diag(bb_im, HG, P)], axis=-1).astype(BF16)
    ct_re = jnp.transpose(c_re, (0, 1, 3, 2)).reshape(depth, S5_NBLK, S5_BLK_ST, HG)
    ct_im = jnp.transpose(c_im, (0, 1, 3, 2)).reshape(depth, S5_NBLK, S5_BLK_ST, HG)
    wc = jnp.concatenate([_block_diag(ct_re, P, HG), -_block_diag(ct_im, P, HG)], axis=-2).astype(BF16)
    return a_r, a_i, wb, wc


def _s5_kernel(u_ref, p_ref, pt_ref, wb_ref, ar_ref, ai_ref, wc_ref, d_ref, y_ref,
               bu_ref, x_ref, sr_ref, si_ref, *, ti, seg_len):
    R = S5_SEGS * ti
    NS = S5_BLK_ST
    ph = pl.program_id(0)
    it = pl.program_id(1)

    @pl.when(jnp.logical_and(ph == 0, it == 0))
    def _():
        sr_ref[...] = jnp.zeros_like(sr_ref)
        si_ref[...] = jnp.zeros_like(si_ref)

    @pl.when(jnp.logical_and(ph == 1, it == 0))
    def _():
        pr, pi = ar_ref[...], ai_ref[...]
        for _ in range(int(math.log2(seg_len))):
            pr, pi = pr * pr - pi * pi, 2.0 * pr * pi
        er, ei = sr_ref[...], si_ref[...]
        row = lax.broadcasted_iota(jnp.int32, er.shape, 0)
        zr = jnp.zeros_like(er)
        zi = jnp.zeros_like(ei)
        for _ in range(S5_SEGS - 1):
            nr = pr * zr - pi * zi + er
            ni = pr * zi + pi * zr + ei
            zr = jnp.where(row >= 1, pltpu.roll(nr, 1, 0), 0.0)
            zi = jnp.where(row >= 1, pltpu.roll(ni, 1, 0), 0.0)
        sr_ref[...] = zr
        si_ref[...] = zi

    u_nat = u_ref[...].reshape(R, S5_WIDTH)
    u2 = jnp.dot(p_ref[...], u_nat.astype(BF16), preferred_element_type=F32).astype(BF16)
    for b in range(S5_NBLK):
        bu_ref[:, b * 2 * NS:(b + 1) * 2 * NS] = jnp.dot(
            u2[:, b * S5_BLK_CH:(b + 1) * S5_BLK_CH], wb_ref[b], preferred_element_type=F32)

    def scan(store):
        for b in range(S5_NBLK):
            ar = ar_ref[:, b * NS:(b + 1) * NS]
            ai = ai_ref[:, b * NS:(b + 1) * NS]
            c_re = b * 2 * NS
            c_im = c_re + NS

            def body(i, carry):
                xr, xi = carry
                r0 = pl.multiple_of(i * SUBLANES, SUBLANES)
                br = bu_ref[pl.ds(r0, SUBLANES), c_re:c_re + NS]
                bi = bu_ref[pl.ds(r0, SUBLANES), c_im:c_im + NS]
                nr = ar * xr - ai * xi + br
                ni = ar * xi + ai * xr + bi
                if store:
                    x_ref[pl.ds(r0, SUBLANES), c_re:c_re + NS] = nr
                    x_ref[pl.ds(r0, SUBLANES), c_im:c_im + NS] = ni
                return nr, ni

            xr, xi = lax.fori_loop(
                0, ti, body, (sr_ref[:, b * NS:(b + 1) * NS], si_ref[:, b * NS:(b + 1) * NS]), unroll=4)
            sr_ref[:, b * NS:(b + 1) * NS] = xr
            si_ref[:, b * NS:(b + 1) * NS] = xi

    @pl.when(ph == 0)
    def _():
        scan(False)

    @pl.when(ph == 1)
    def _():
        scan(True)
        ys = []
        for b in range(S5_NBLK):
            xb = x_ref[:, b * 2 * NS:(b + 1) * 2 * NS].astype(BF16)
            ys.append(jnp.dot(xb, wc_ref[b], preferred_element_type=F32))
        y2 = jnp.concatenate(ys, axis=1)
        y_hi = y2.astype(BF16)
        y_lo = (y2 - y_hi.astype(F32)).astype(BF16)
        pt = pt_ref[...]
        y_nat = (jnp.dot(pt, y_hi, preferred_element_type=F32)
                 + jnp.dot(pt, y_lo, preferred_element_type=F32))
        y_nat = y_nat + d_ref[...] * u_nat
        y = 0.5 * y_nat * (1.0 + lax.erf(y_nat * (1.0 / math.sqrt(2.0))))
        y_ref[...] = y.reshape(S5_SEGS, ti, S5_WIDTH).astype(y_ref.dtype)


def s5_perm(ti):
    R = S5_SEGS * ti
    r = lax.broadcasted_iota(jnp.int32, (R, R), 0)
    c = lax.broadcasted_iota(jnp.int32, (R, R), 1)
    perm = (c == (r % S5_SEGS) * ti + r // S5_SEGS).astype(BF16)
    return perm, perm.T


def s5_scan(proj, perm, perm_t, a_r, a_i, wb, wc, d, layer, ti=32):
    L = proj.shape[0]
    seg_len = L // S5_SEGS
    assert seg_len * S5_SEGS == L and seg_len % ti == 0 and (seg_len & (seg_len - 1)) == 0
    R = S5_SEGS * ti
    nt = seg_len // ti
    proj3 = proj.reshape(S5_SEGS, seg_len, proj.shape[1])
    ns = S5_GROUPS * S5_STATE
    kern = functools.partial(_s5_kernel, ti=ti, seg_len=seg_len)
    y = pl.pallas_call(
        kern,
        grid=(2, nt),
        in_specs=[pl.BlockSpec((S5_SEGS, ti, S5_WIDTH), lambda p, t: (0, t, 0)),
                  pl.BlockSpec((R, R), lambda p, t: (0, 0)),
                  pl.BlockSpec((R, R), lambda p, t: (0, 0)),
                  pl.BlockSpec((None, S5_NBLK, S5_BLK_CH, 2 * S5_BLK_ST), lambda p, t: (layer, 0, 0, 0)),
                  pl.BlockSpec((None, S5_SEGS, ns), lambda p, t: (layer, 0, 0)),
                  pl.BlockSpec((None, S5_SEGS, ns), lambda p, t: (layer, 0, 0)),
                  pl.BlockSpec((None, S5_NBLK, 2 * S5_BLK_ST, S5_BLK_CH), lambda p, t: (layer, 0, 0, 0)),
                  pl.BlockSpec((None, 1, S5_WIDTH), lambda p, t: (layer, 0, 0))],
        out_specs=pl.BlockSpec((S5_SEGS, ti, S5_WIDTH), lambda p, t: (0, t * p, 0)),
        out_shape=jax.ShapeDtypeStruct((S5_SEGS, seg_len, S5_WIDTH), BF16),
        scratch_shapes=[pltpu.VMEM((R, 2 * ns), F32),
                        pltpu.VMEM((R, 2 * ns), F32),
                        pltpu.VMEM((S5_SEGS, ns), F32),
                        pltpu.VMEM((S5_SEGS, ns), F32)],
        compiler_params=_cparams(("arbitrary", "arbitrary")),
        name="s5_scan",
    )(proj3, perm, perm_t, wb, a_r, a_i, wc, d.reshape(-1, 1, S5_WIDTH))
    return y.reshape(L, S5_WIDTH)


def _dot_nt(a, b):
    return lax.dot_general(a, b, (((1,), (1,)), ((), ())), preferred_element_type=F32)


def _dot_tn(a, b, precision=None):
    return lax.dot_general(a, b, (((0,), (0,)), ((), ())), preferred_element_type=F32,
                           precision=precision)


def _dn_kernel(q_ref, k_ref, v_ref, z_ref, ba_ref, cw_ref, alog_ref, dtb_ref, nw_ref, o_ref,
               ext_ref, act_ref, gate_ref, s_ref, *, tm):
    C = DN_CHUNK
    W = DN_QK_WIDTH
    it = pl.program_id(0)

    @pl.when(it == 0)
    def _():
        ext_ref[0:SUBLANES, :] = jnp.zeros((SUBLANES, 3 * W), F32)
        s_ref[...] = jnp.zeros_like(s_ref)

    ext_ref[SUBLANES:SUBLANES + tm, 0:W] = q_ref[...]
    ext_ref[SUBLANES:SUBLANES + tm, W:2 * W] = k_ref[...]
    ext_ref[SUBLANES:SUBLANES + tm, 2 * W:3 * W] = v_ref[...]
    cw = cw_ref[...]
    for cb in range(3 * W // LANES):
        cols = slice(cb * LANES, (cb + 1) * LANES)
        acc = None
        for j in range(DN_CONV):
            off = SUBLANES - (DN_CONV - 1) + j
            term = cw[j:j + 1, cols] * ext_ref[off:off + tm, cols]
            acc = term if acc is None else acc + term
        act_ref[:, cols] = _silu(acc)
    ext_ref[0:SUBLANES, :] = ext_ref[tm:tm + SUBLANES, :]

    ba = ba_ref[...]
    lane = lax.broadcasted_iota(jnp.int32, ba.shape, 1)
    beta = _sigmoid(ba)
    g = -jnp.exp(alog_ref[...]) * _softplus(ba + dtb_ref[...])
    gate_ref[...] = jnp.where(lane < DN_HEADS, beta, g)

    ri = lax.broadcasted_iota(jnp.int32, (C, C), 0)
    ci = lax.broadcasted_iota(jnp.int32, (C, C), 1)
    tril = ri >= ci
    strict = ri > ci
    tril_f = tril.astype(F32)
    triu_f = (ri <= ci).astype(F32)
    eye_f = (ri == ci).astype(F32)
    blk = DN_SOLVE_BLOCK
    same_blk = (ri // blk) == (ci // blk)
    sub_blk = ((ri // blk) == (ci // blk) + 1) & ((ri // (2 * blk)) == (ci // (2 * blk)))
    low_half = (ri >= C // 2) & (ci < C // 2)
    nw = nw_ref[...]
    heads = range(DN_HEADS)

    def bdot(a, b):
        return jnp.dot(a.astype(BF16), b.astype(BF16), preferred_element_type=F32)

    def chunk(c, carry):
        r0 = pl.multiple_of(c * C, C)
        gt = gate_ref[pl.ds(r0, C), :]
        gc = jnp.dot(tril_f, gt, preferred_element_type=F32, precision=HI)
        gc_t = _dot_tn(gt, triu_f, precision=HI)
        q, k, v, kb, vb, gcol, glast, eg, dec = [], [], [], [], [], [], [], [], []
        for h in heads:
            lo = h * DN_DK
            qh = act_ref[pl.ds(r0, C), lo:lo + DN_DK]
            kh = act_ref[pl.ds(r0, C), W + lo:W + lo + DN_DK]
            vh = act_ref[pl.ds(r0, C), 2 * W + lo:2 * W + lo + DN_DV]
            q.append(qh * lax.rsqrt(jnp.sum(qh * qh, axis=-1, keepdims=True) + NORM_EPS) * (DN_DK ** -0.5))
            kh = kh * lax.rsqrt(jnp.sum(kh * kh, axis=-1, keepdims=True) + NORM_EPS)
            k.append(kh)
            bcol = gt[:, h:h + 1]
            gc_h = gc[:, DN_HEADS + h:DN_HEADS + h + 1]
            grow = gc_t[DN_HEADS + h:DN_HEADS + h + 1, :]
            gcol.append(gc_h)
            glast.append(gc_h[C - 1:C, :])
            eg.append(jnp.exp(gc_h))
            dec.append(jnp.where(tril, jnp.exp(jnp.where(tril, gc_h - grow, 0.0)), 0.0))
            kb.append(kh * bcol)
            vb.append(vh * bcol)
            v.append(vh)
        k16 = [k[h].astype(BF16) for h in heads]
        kk = [_dot_nt(kb[h].astype(BF16), k16[h]) for h in heads]
        qk = [_dot_nt(q[h].astype(BF16), k16[h]) for h in heads]
        lmat = [jnp.where(strict, kk[h] * dec[h], 0.0) for h in heads]
        attn = [(qk[h] * dec[h]).astype(BF16) for h in heads]

        def split(x):
            hi = x.astype(BF16)
            return hi, (x - hi.astype(F32)).astype(BF16)

        def dot3(a, b):
            (ah, al), (bh, bl) = a, b
            return (jnp.dot(ah, bh, preferred_element_type=F32)
                    + jnp.dot(al, bh, preferred_element_type=F32)
                    + jnp.dot(ah, bl, preferred_element_type=F32))

        pw = [jnp.where(same_blk, -lmat[h], 0.0) for h in heads]
        tinv = [eye_f + pw[h] for h in heads]
        pws = [split(pw[h]) for h in heads]
        for _ in range(int(math.log2(blk)) - 1):
            pw = [dot3(pws[h], pws[h]) for h in heads]
            pws = [split(pw[h]) for h in heads]
            tinv = [tinv[h] + dot3(split(tinv[h]), pws[h]) for h in heads]
        x1 = [bdot(jnp.where(sub_blk, lmat[h], 0.0), tinv[h]) for h in heads]
        tinv = [tinv[h] - bdot(tinv[h], x1[h]) for h in heads]
        x2 = [bdot(jnp.where(low_half, lmat[h], 0.0), tinv[h]) for h in heads]
        tinv = [tinv[h] - bdot(tinv[h], x2[h]) for h in heads]

        rhs = [jnp.concatenate([vb[h], kb[h] * eg[h]], axis=1) for h in heads]
        sol = [bdot(tinv[h], rhs[h]) for h in heads]
        s = [s_ref[h] for h in heads]
        s16 = [s[h].astype(BF16) for h in heads]
        wq = [jnp.concatenate([sol[h][:, DN_DV:], q[h] * eg[h]], axis=0).astype(BF16) for h in heads]
        wqs = [jnp.dot(wq[h], s16[h], preferred_element_type=F32) for h in heads]
        vn16 = [(sol[h][:, :DN_DV] - wqs[h][:C, :]).astype(BF16) for h in heads]
        kd = [(k[h] * jnp.exp(glast[h] - gcol[h])).astype(BF16) for h in heads]
        o = [wqs[h][C:, :] + jnp.dot(attn[h], vn16[h], preferred_element_type=F32) for h in heads]
        for h in heads:
            s_ref[h] = s[h] * jnp.exp(glast[h]) + _dot_tn(kd[h], vn16[h])
        for h in heads:
            lo = h * DN_DV
            zh = z_ref[pl.ds(r0, C), lo:lo + DN_DV]
            oh = o[h]
            on = oh * lax.rsqrt(jnp.mean(oh * oh, axis=-1, keepdims=True) + NORM_EPS) * nw
            o_ref[pl.ds(r0, C), lo:lo + DN_DV] = (on * _silu(zh)).astype(o_ref.dtype)
        return carry

    lax.fori_loop(0, tm // C, chunk, 0, unroll=2)


def deltanet(proj, ba, conv_w, a_log, dt_bias, norm_w, tm=256):
    L = proj.shape[0]
    W = DN_QK_WIDTH
    pad = jnp.zeros((1, LANES), F32)
    alog_row = pad.at[0, DN_HEADS:2 * DN_HEADS].set(a_log)
    dtb_row = pad.at[0, DN_HEADS:2 * DN_HEADS].set(dt_bias)
    kern = functools.partial(_dn_kernel, tm=tm)
    return pl.pallas_call(
        kern,
        grid=(L // tm,),
        in_specs=[pl.BlockSpec((tm, W), lambda t: (t, 1)),
                  pl.BlockSpec((tm, W), lambda t: (t, 2)),
                  pl.BlockSpec((tm, W), lambda t: (t, 3)),
                  pl.BlockSpec((tm, W), lambda t: (t, 4)),
                  pl.BlockSpec((tm, LANES), lambda t: (t, 0)),
                  pl.BlockSpec((DN_CONV, 3 * W), lambda t: (0, 0)),
                  pl.BlockSpec((1, LANES), lambda t: (0, 0)),
                  pl.BlockSpec((1, LANES), lambda t: (0, 0)),
                  pl.BlockSpec((1, DN_DV), lambda t: (0, 0))],
        out_specs=pl.BlockSpec((tm, DN_V_WIDTH), lambda t: (t, 0)),
        out_shape=jax.ShapeDtypeStruct((L, DN_V_WIDTH), BF16),
        scratch_shapes=[pltpu.VMEM((tm + SUBLANES, 3 * W), F32),
                        pltpu.VMEM((tm, 3 * W), F32),
                        pltpu.VMEM((tm, LANES), F32),
                        pltpu.VMEM((DN_HEADS, DN_DK, DN_DV), F32)],
        compiler_params=_cparams(("arbitrary",)),
        name="deltanet",
    )(proj, proj, proj, proj, ba, conv_w, alog_row, dtb_row, norm_w.reshape(1, DN_DV))


def _merge_kernel(ys_ref, yd_ref, wga_ref, wgb_ref, wdn_ref, gs_ref, gd_ref, o_ref,
                  wga16_ref, wgb16_ref, wdn16_ref):
    @pl.when(pl.program_id(1) == 0)
    def _():
        wga16_ref[...] = wga_ref[...].astype(BF16)
        wgb16_ref[...] = wgb_ref[...].astype(BF16)
        wdn16_ref[...] = wdn_ref[...].astype(BF16)

    ys = ys_ref[...]
    ga = jnp.dot(ys, wga16_ref[...], preferred_element_type=F32)
    gb = jnp.dot(ys, wgb16_ref[...], preferred_element_type=F32)
    dn = jnp.dot(yd_ref[...], wdn16_ref[...], preferred_element_type=F32)
    br_s5 = ga * _sigmoid(gb)
    o_ref[...] = (_sigmoid(gs_ref[...]) * br_s5 + _sigmoid(gd_ref[...]) * dn).astype(o_ref.dtype)


def merge(y_s5, y_dn, glu_w, dn_w, gates, layer, tm=1024, tn=512):
    L = y_s5.shape[0]
    nb = D_MODEL // tn
    gs_blk = 0
    gd_blk = nb
    return pl.pallas_call(
        _merge_kernel,
        grid=(nb, L // tm),
        in_specs=[pl.BlockSpec((tm, S5_WIDTH), lambda n, m: (m, 0)),
                  pl.BlockSpec((tm, DN_V_WIDTH), lambda n, m: (m, 0)),
                  pl.BlockSpec((None, S5_WIDTH, tn), lambda n, m: (layer, 0, n)),
                  pl.BlockSpec((None, S5_WIDTH, tn), lambda n, m: (layer, 0, n + nb)),
                  pl.BlockSpec((None, DN_V_WIDTH, tn), lambda n, m: (layer, 0, n)),
                  pl.BlockSpec((tm, tn), lambda n, m: (m, n + gs_blk)),
                  pl.BlockSpec((tm, tn), lambda n, m: (m, n + gd_blk))],
        out_specs=pl.BlockSpec((tm, tn), lambda n, m: (m, n)),
        out_shape=jax.ShapeDtypeStruct((L, D_MODEL), BF16),
        scratch_shapes=[pltpu.VMEM((S5_WIDTH, tn), BF16),
                        pltpu.VMEM((S5_WIDTH, tn), BF16),
                        pltpu.VMEM((DN_V_WIDTH, tn), BF16)],
        compiler_params=_cparams(("parallel", "arbitrary")),
        name="merge",
    )(y_s5, y_dn, glu_w, glu_w, dn_w, gates, gates)


def _out_proj_kernel(m_ref, w_ref, x_ref, nw_ref, xo_ref, h_ref, w16_ref):
    @pl.when(pl.program_id(0) == 0)
    def _():
        w16_ref[...] = w_ref[...].astype(BF16)

    xo = x_ref[...] + jnp.dot(m_ref[...], w16_ref[...], preferred_element_type=F32)
    xo_ref[...] = xo
    h_ref[...] = _rms_rows(xo, nw_ref[...]).astype(h_ref.dtype)


def out_proj(merged, w_out, x, norm_w, layer, tm=512):
    L, D = x.shape
    return pl.pallas_call(
        _out_proj_kernel,
        grid=(L // tm,),
        in_specs=[pl.BlockSpec((tm, D), lambda m: (m, 0)),
                  pl.BlockSpec((None, D, D), lambda m: (layer, 0, 0), pipeline_mode=pl.Buffered(1)),
                  pl.BlockSpec((tm, D), lambda m: (m, 0)),
                  pl.BlockSpec((1, D), lambda m: (0, 0))],
        out_specs=[pl.BlockSpec((tm, D), lambda m: (m, 0)),
                   pl.BlockSpec((tm, D), lambda m: (m, 0))],
        out_shape=[jax.ShapeDtypeStruct((L, D), F32),
                   jax.ShapeDtypeStruct((L, D), BF16)],
        scratch_shapes=[pltpu.VMEM((D, D), BF16)],
        compiler_params=_cparams(("arbitrary",)),
        name="out_proj",
    )(merged, w_out, x, norm_w.reshape(1, D))


def _ffn_up_kernel(h_ref, wa_ref, wv_ref, cwa_ref, cwv_ref, o_ref, ext_ref, w16_ref, *, tm, tn):
    @pl.when(pl.program_id(1) == 0)
    def _():
        ext_ref[...] = jnp.zeros_like(ext_ref)
        w16_ref[:, :tn] = wa_ref[...].astype(BF16)
        w16_ref[:, tn:] = wv_ref[...].astype(BF16)

    up = jnp.dot(h_ref[...], w16_ref[...], preferred_element_type=F32)
    cw = jnp.concatenate([cwa_ref[...], cwv_ref[...]], axis=1)
    prev = ext_ref[...]
    r8 = lax.broadcasted_iota(jnp.int32, prev.shape, 0)
    acc = cw[FFN_CONV - 1:FFN_CONV, :] * up
    for back in range(1, FFN_CONV):
        sh = pltpu.roll(up, back, 0)
        head = jnp.where(r8 < back, pltpu.roll(prev, back, 0), sh[:SUBLANES])
        sh = jnp.concatenate([head, sh[SUBLANES:]], axis=0)
        acc = acc + cw[FFN_CONV - 1 - back:FFN_CONV - back, :] * sh
    o_ref[...] = (_silu(acc[:, :tn]) * acc[:, tn:]).astype(o_ref.dtype)
    ext_ref[...] = up[tm - SUBLANES:, :]


def ffn_up_call(h, w_up, conv_w, layer, tm=1024, tn=512):
    L, D = h.shape
    nb = FFN_DIM // tn
    kern = functools.partial(_ffn_up_kernel, tm=tm, tn=tn)
    return pl.pallas_call(
        kern,
        grid=(nb, L // tm),
        in_specs=[pl.BlockSpec((tm, D), lambda n, m: (m, 0)),
                  pl.BlockSpec((None, D, tn), lambda n, m: (layer, 0, n)),
                  pl.BlockSpec((None, D, tn), lambda n, m: (layer, 0, n + nb)),
                  pl.BlockSpec((FFN_CONV, tn), lambda n, m: (0, n)),
                  pl.BlockSpec((FFN_CONV, tn), lambda n, m: (0, n + nb))],
        out_specs=pl.BlockSpec((tm, tn), lambda n, m: (m, n)),
        out_shape=jax.ShapeDtypeStruct((L, FFN_DIM), BF16),
        scratch_shapes=[pltpu.VMEM((SUBLANES, 2 * tn), F32),
                        pltpu.VMEM((D, 2 * tn), BF16)],
        compiler_params=_cparams(("parallel", "arbitrary")),
        name="ffn_up",
    )(h, w_up, w_up, conv_w, conv_w)


def _ffn_down_kernel(a_ref, w_ref, x_ref, nw_ref, *refs, last):
    acc_ref = refs[0]
    k = pl.program_id(1)

    @pl.when(k == 0)
    def _():
        acc_ref[...] = x_ref[...]

    acc_ref[...] += jnp.dot(a_ref[...], w_ref[...], preferred_element_type=F32)

    @pl.when(k == pl.num_programs(1) - 1)
    def _():
        xo = acc_ref[...]
        if last:
            acc_ref[...] = _rms_rows(xo, nw_ref[...])
        else:
            refs[1][...] = _rms_rows(xo, nw_ref[...]).astype(refs[1].dtype)


def ffn_down_call(a, w_down, x, norm_w, layer, last, tm=1024, tk=512):
    L, D = x.shape
    K = a.shape[1]
    row = pl.BlockSpec((tm, D), lambda m, k: (m, 0))
    if last:
        out_specs = [row]
        out_shape = [jax.ShapeDtypeStruct((L, D), F32)]
    else:
        out_specs = [row, row]
        out_shape = [jax.ShapeDtypeStruct((L, D), F32), jax.ShapeDtypeStruct((L, D), BF16)]
    kern = functools.partial(_ffn_down_kernel, last=last)
    return pl.pallas_call(
        kern,
        grid=(L // tm, K // tk),
        in_specs=[pl.BlockSpec((tm, tk), lambda m, k: (m, k)),
                  pl.BlockSpec((None, tk, D), lambda m, k: (layer, k, 0)),
                  row,
                  pl.BlockSpec((1, D), lambda m, k: (0, 0))],
        out_specs=out_specs,
        out_shape=out_shape,
        compiler_params=_cparams(("parallel", "arbitrary")),
        name="ffn_down",
    )(a, w_down, x, norm_w.reshape(1, D))


def kernel(x, mix_norm_w, w_in, s5_log_dt, s5_a_re, s5_a_im, s5_b_re, s5_b_im, s5_c_re, s5_c_im, s5_d, s5_glu_w, dn_conv_w, dn_a_log, dn_dt_bias, dn_norm_w, dn_proj_w, w_out, ffn_norm_w, ffn_up, ffn_conv_w, ffn_down, final_norm_w):
    B, L, D = x.shape
    depth = w_in.shape[0]
    assert B == 1 and D == D_MODEL
    xs = x.reshape(L, D)
    h = rmsnorm_cast(xs, mix_norm_w[0])
    abr, abi, bbr, bbi = s5_discretise(s5_log_dt, s5_a_re, s5_a_im, s5_b_re, s5_b_im)
    s5_ar, s5_ai, s5_wb, s5_wc = s5_layout_params(abr, abi, bbr, bbi, s5_c_re, s5_c_im, depth)
    s5_ti = 32
    perm, perm_t = s5_perm(s5_ti)
    w_mix = w_in[:, :, :OFF_Z].astype(BF16)
    w_gates = w_in[:, :, OFF_ALPHA:].astype(BF16)
    w_ba = jnp.pad(w_in[:, :, OFF_Z:OFF_ALPHA], ((0, 0), (0, 0), (0, LANES - 2 * DN_HEADS))).astype(BF16)
    w_down = ffn_down.astype(BF16)
    for l in range(depth):
        proj = in_proj_mix(h, w_mix, l)
        gates, ba = in_proj_gates(h, w_gates, w_ba, l)

        y_s5 = s5_scan(proj, perm, perm_t, s5_ar, s5_ai, s5_wb, s5_wc, s5_d, l, ti=s5_ti)
        y_dn = deltanet(proj, ba, dn_conv_w[l], dn_a_log[l], dn_dt_bias[l], dn_norm_w[l])

        merged = merge(y_s5, y_dn, s5_glu_w, dn_proj_w, gates, l)
        xs, h = out_proj(merged, w_out, xs, ffn_norm_w[l], l)
        act = ffn_up_call(h, ffn_up, ffn_conv_w[l], l)
        if l + 1 < depth:
            xs, h = ffn_down_call(act, w_down, xs, mix_norm_w[l + 1], l, last=False)
        else:
            (out,) = ffn_down_call(act, w_down, xs, final_norm_w, l, last=True)
    return out.reshape(B, L, D)
```

```python
import functools
import math

import jax
import jax.numpy as jnp
from jax import lax
from jax.experimental import pallas as pl
from jax.experimental.pallas import tpu as pltpu

F32 = jnp.float32
BF16 = jnp.bfloat16

D_MODEL = 2048
S5_WIDTH = 1024
S5_GROUP = 16
S5_GROUPS = 64
S5_STATE = 64
DN_HEADS = 8
DN_DK = 128
DN_DV = 128
DN_QK_WIDTH = DN_HEADS * DN_DK
DN_V_WIDTH = DN_HEADS * DN_DV
DN_CONV = 4
DN_CHUNK = 64
DN_SOLVE_BLOCK = 16
FFN_DIM = 5632
FFN_CONV = 3
NORM_EPS = 1e-6

OFF_U = S5_WIDTH
OFF_QKV = OFF_U + 2 * DN_QK_WIDTH + DN_V_WIDTH
OFF_Z = OFF_QKV + DN_V_WIDTH
OFF_BETA = OFF_Z + DN_HEADS
OFF_ALPHA = OFF_BETA + DN_HEADS
OFF_GS = OFF_ALPHA + D_MODEL
N_IN = OFF_GS + D_MODEL
N_MAIN = N_IN - 2 * DN_HEADS

LANES = 128
SUBLANES = 8
VMEM_LIMIT = 56 * 1024 * 1024

S5_SEGS = SUBLANES
S5_GB = 8
S5_NBLK = S5_GROUPS // S5_GB
S5_BLK_CH = S5_GB * S5_GROUP
S5_BLK_ST = S5_GB * S5_STATE

HI = lax.Precision.HIGHEST


def _cparams(sem):
    return pltpu.CompilerParams(dimension_semantics=sem, vmem_limit_bytes=VMEM_LIMIT)


def _sigmoid(x):
    return 1.0 / (1.0 + jnp.exp(-x))


def _silu(x):
    hx = 0.5 * x
    return hx + hx * jnp.tanh(hx)


def _softplus(x):
    return jnp.maximum(x, 0.0) + jnp.log1p(jnp.exp(-jnp.abs(x)))


def _rms_rows(x, w):
    ms = jnp.mean(x * x, axis=-1, keepdims=True)
    return x * lax.rsqrt(ms + NORM_EPS) * w


def _rmsnorm_kernel(x_ref, w_ref, o_ref):
    o_ref[...] = _rms_rows(x_ref[...], w_ref[...]).astype(o_ref.dtype)


def rmsnorm_cast(x, w, tm=512):
    L, D = x.shape
    return pl.pallas_call(
        _rmsnorm_kernel,
        grid=(L // tm,),
        in_specs=[pl.BlockSpec((tm, D), lambda m: (m, 0)),
                  pl.BlockSpec((1, D), lambda m: (0, 0))],
        out_specs=pl.BlockSpec((tm, D), lambda m: (m, 0)),
        out_shape=jax.ShapeDtypeStruct((L, D), BF16),
        compiler_params=_cparams(("parallel",)),
        name="rmsnorm",
    )(x, w.reshape(1, D))


def _dot_nt(a, b):
    return lax.dot_general(a, b, (((1,), (1,)), ((), ())), preferred_element_type=F32)


def _in_proj_kernel(h_ref, wt_ref, o_ref, w16_ref):
    @pl.when(pl.program_id(1) == 0)
    def _():
        w16_ref[...] = wt_ref[...].astype(BF16)

    o_ref[...] = _dot_nt(h_ref[...], w16_ref[...])


def in_proj(h, w_in_t, layer, tm=1024, tn=1024):
    L, D = h.shape
    n_mix = OFF_Z // tn

    def w_index(n, m):
        row = n * tn + jnp.where(n >= n_mix, OFF_ALPHA - OFF_Z, 0)
        return layer, pl.multiple_of(row, 2 * DN_HEADS), 0

    return pl.pallas_call(
        _in_proj_kernel,
        grid=(N_MAIN // tn, L // tm),
        in_specs=[pl.BlockSpec((tm, D), lambda n, m: (m, 0)),
                  pl.BlockSpec((None, pl.Element(tn), pl.Element(D)), w_index)],
        out_specs=pl.BlockSpec((tm, tn), lambda n, m: (m, n)),
        out_shape=jax.ShapeDtypeStruct((L, N_MAIN), F32),
        scratch_shapes=[pltpu.VMEM((tn, D), BF16)],
        compiler_params=_cparams(("parallel", "arbitrary")),
        name="in_proj",
    )(h, w_in_t)


def _s5_disc_kernel(ldt_ref, are_ref, aim_ref, bre_ref, bim_ref,
                    abr_ref, abi_ref, bbr_ref, bbi_ref):
    lr = are_ref[...]
    li = aim_ref[...]
    dt = jnp.exp(ldt_ref[...])
    mag = jnp.exp(lr * dt)
    abar_re = mag * jnp.cos(li * dt)
    abar_im = mag * jnp.sin(li * dt)
    den = lr * lr + li * li
    nr = abar_re - 1.0
    ni = abar_im
    coef_re = (nr * lr + ni * li) / den
    coef_im = (ni * lr - nr * li) / den
    br = bre_ref[...]
    bi = bim_ref[...]
    abr_ref[...] = abar_re
    abi_ref[...] = abar_im
    bbr_ref[...] = coef_re * br - coef_im * bi
    bbi_ref[...] = coef_re * bi + coef_im * br


def s5_discretise(log_dt, a_re, a_im, b_re, b_im):
    P, HG = S5_STATE, S5_GROUP
    dg = log_dt.size
    bt_re = jnp.transpose(b_re.reshape(dg, P, HG), (0, 2, 1))
    bt_im = jnp.transpose(b_im.reshape(dg, P, HG), (0, 2, 1))
    return pl.pallas_call(
        _s5_disc_kernel,
        out_shape=[jax.ShapeDtypeStruct((dg, 1, P), F32),
                   jax.ShapeDtypeStruct((dg, 1, P), F32),
                   jax.ShapeDtypeStruct((dg, HG, P), F32),
                   jax.ShapeDtypeStruct((dg, HG, P), F32)],
        name="s5_disc",
    )(jnp.broadcast_to(log_dt.reshape(dg, 1, 1), (dg, 1, P)), a_re.reshape(dg, 1, P), a_im.reshape(dg, 1, P),
      bt_re, bt_im)


def _block_diag(t, rows_per_blk, cols_per_blk):
    tiled = jnp.tile(t, (1,) * (t.ndim - 1) + (S5_GB,))
    r = lax.broadcasted_iota(jnp.int32, tiled.shape, t.ndim - 2) // rows_per_blk
    c = lax.broadcasted_iota(jnp.int32, tiled.shape, t.ndim - 1) // cols_per_blk
    return jnp.where(r == c, tiled, jnp.zeros_like(tiled))


def s5_layout_params(abar_re, abar_im, bbar_re, bbar_im, c_re, c_im, depth):
    P, HG = S5_STATE, S5_GROUP
    ns = S5_GROUPS * P
    a_r = jnp.broadcast_to(abar_re.reshape(depth, 1, ns), (depth, S5_SEGS, ns))
    a_i = jnp.broadcast_to(abar_im.reshape(depth, 1, ns), (depth, S5_SEGS, ns))
    bb_re = bbar_re.reshape(depth, S5_NBLK, S5_BLK_CH, P)
    bb_im = bbar_im.reshape(depth, S5_NBLK, S5_BLK_CH, P)
    wb = jnp.concatenate([_block_diag(bb_re, HG, P), _block_diag(bb_im, HG, P)], axis=-1).astype(BF16)
    ct_re = jnp.transpose(c_re, (0, 1, 3, 2)).reshape(depth, S5_NBLK, S5_BLK_ST, HG)
    ct_im = jnp.transpose(c_im, (0, 1, 3, 2)).reshape(depth, S5_NBLK, S5_BLK_ST, HG)
    wc = jnp.concatenate([_block_diag(ct_re, P, HG), -_block_diag(ct_im, P, HG)], axis=-2).astype(BF16)
    return a_r, a_i, wb, wc


def _s5_kernel(u_ref, p_ref, pt_ref, wb_ref, ar_ref, ai_ref, wc_ref, d_ref, y_ref,
               bu_ref, x_ref, sr_ref, si_ref, *, ti, seg_len):
    R = S5_SEGS * ti
    NS = S5_BLK_ST
    ph = pl.program_id(0)
    it = pl.program_id(1)

    @pl.when(jnp.logical_and(ph == 0, it == 0))
    def _():
        sr_ref[...] = jnp.zeros_like(sr_ref)
        si_ref[...] = jnp.zeros_like(si_ref)

    @pl.when(jnp.logical_and(ph == 1, it == 0))
    def _():
        pr, pi = ar_ref[...], ai_ref[...]
        for _ in range(int(math.log2(seg_len))):
            pr, pi = pr * pr - pi * pi, 2.0 * pr * pi
        er, ei = sr_ref[...], si_ref[...]
        row = lax.broadcasted_iota(jnp.int32, er.shape, 0)
        zr = jnp.zeros_like(er)
        zi = jnp.zeros_like(ei)
        for _ in range(S5_SEGS - 1):
            nr = pr * zr - pi * zi + er
            ni = pr * zi + pi * zr + ei
            zr = jnp.where(row >= 1, pltpu.roll(nr, 1, 0), 0.0)
            zi = jnp.where(row >= 1, pltpu.roll(ni, 1, 0), 0.0)
        sr_ref[...] = zr
        si_ref[...] = zi

    u_nat = u_ref[...].reshape(R, S5_WIDTH)
    u2 = jnp.dot(p_ref[...], u_nat.astype(BF16), preferred_element_type=F32).astype(BF16)
    for b in range(S5_NBLK):
        bu_ref[:, b * 2 * NS:(b + 1) * 2 * NS] = jnp.dot(
            u2[:, b * S5_BLK_CH:(b + 1) * S5_BLK_CH], wb_ref[b], preferred_element_type=F32)

    def scan(store):
        for b in range(S5_NBLK):
            ar = ar_ref[:, b * NS:(b + 1) * NS]
            ai = ai_ref[:, b * NS:(b + 1) * NS]
            c_re = b * 2 * NS
            c_im = c_re + NS

            def body(i, carry):
                xr, xi = carry
                r0 = pl.multiple_of(i * SUBLANES, SUBLANES)
                br = bu_ref[pl.ds(r0, SUBLANES), c_re:c_re + NS]
                bi = bu_ref[pl.ds(r0, SUBLANES), c_im:c_im + NS]
                nr = ar * xr - ai * xi + br
                ni = ar * xi + ai * xr + bi
                if store:
                    x_ref[pl.ds(r0, SUBLANES), c_re:c_re + NS] = nr
                    x_ref[pl.ds(r0, SUBLANES), c_im:c_im + NS] = ni
                return nr, ni

            xr, xi = lax.fori_loop(
                0, ti, body, (sr_ref[:, b * NS:(b + 1) * NS], si_ref[:, b * NS:(b + 1) * NS]), unroll=4)
            sr_ref[:, b * NS:(b + 1) * NS] = xr
            si_ref[:, b * NS:(b + 1) * NS] = xi

    @pl.when(ph == 0)
    def _():
        scan(False)

    @pl.when(ph == 1)
    def _():
        scan(True)
        ys = []
        for b in range(S5_NBLK):
            xb = x_ref[:, b * 2 * NS:(b + 1) * 2 * NS].astype(BF16)
            ys.append(jnp.dot(xb, wc_ref[b], preferred_element_type=F32))
        y2 = jnp.concatenate(ys, axis=1)
        y_hi = y2.astype(BF16)
        y_lo = (y2 - y_hi.astype(F32)).astype(BF16)
        pt = pt_ref[...]
        y_nat = (jnp.dot(pt, y_hi, preferred_element_type=F32)
                 + jnp.dot(pt, y_lo, preferred_element_type=F32))
        y_nat = y_nat + d_ref[...] * u_nat
        y = 0.5 * y_nat * (1.0 + lax.erf(y_nat * (1.0 / math.sqrt(2.0))))
        y_ref[...] = y.reshape(S5_SEGS, ti, S5_WIDTH).astype(y_ref.dtype)


def s5_perm(ti):
    R = S5_SEGS * ti
    r = lax.broadcasted_iota(jnp.int32, (R, R), 0)
    c = lax.broadcasted_iota(jnp.int32, (R, R), 1)
    perm = (c == (r % S5_SEGS) * ti + r // S5_SEGS).astype(BF16)
    return perm, perm.T


def s5_scan(proj, perm, perm_t, a_r, a_i, wb, wc, d, layer, ti=32):
    L = proj.shape[0]
    seg_len = L // S5_SEGS
    assert seg_len * S5_SEGS == L and seg_len % ti == 0 and (seg_len & (seg_len - 1)) == 0
    R = S5_SEGS * ti
    nt = seg_len // ti
    proj3 = proj.reshape(S5_SEGS, seg_len, proj.shape[1])
    ns = S5_GROUPS * S5_STATE
    kern = functools.partial(_s5_kernel, ti=ti, seg_len=seg_len)
    y = pl.pallas_call(
        kern,
        grid=(2, nt),
        in_specs=[pl.BlockSpec((S5_SEGS, ti, S5_WIDTH), lambda p, t: (0, t, 0)),
                  pl.BlockSpec((R, R), lambda p, t: (0, 0)),
                  pl.BlockSpec((R, R), lambda p, t: (0, 0)),
                  pl.BlockSpec((None, S5_NBLK, S5_BLK_CH, 2 * S5_BLK_ST), lambda p, t: (layer, 0, 0, 0)),
                  pl.BlockSpec((None, S5_SEGS, ns), lambda p, t: (layer, 0, 0)),
                  pl.BlockSpec((None, S5_SEGS, ns), lambda p, t: (layer, 0, 0)),
                  pl.BlockSpec((None, S5_NBLK, 2 * S5_BLK_ST, S5_BLK_CH), lambda p, t: (layer, 0, 0, 0)),
                  pl.BlockSpec((None, 1, S5_WIDTH), lambda p, t: (layer, 0, 0))],
        out_specs=pl.BlockSpec((S5_SEGS, ti, S5_WIDTH), lambda p, t: (0, t * p, 0)),
        out_shape=jax.ShapeDtypeStruct((S5_SEGS, seg_len, S5_WIDTH), BF16),
        scratch_shapes=[pltpu.VMEM((R, 2 * ns), F32),
                        pltpu.VMEM((R, 2 * ns), F32),
                        pltpu.VMEM((S5_SEGS, ns), F32),
                        pltpu.VMEM((S5_SEGS, ns), F32)],
        compiler_params=_cparams(("arbitrary", "arbitrary")),
        name="s5_scan",
    )(proj3, perm, perm_t, wb, a_r, a_i, wc, d.reshape(-1, 1, S5_WIDTH))
    return y.reshape(L, S5_WIDTH)


def _dot_tn(a, b, precision=None):
    return lax.dot_general(a, b, (((0,), (0,)), ((), ())), preferred_element_type=F32,
                           precision=precision)


def _dn_kernel(q_ref, k_ref, v_ref, z_ref, h_ref, wba_ref, cw_ref, alog_ref, dtb_ref, nw_ref, o_ref,
               ext_ref, act_ref, gate_ref, s_ref, wba16_ref, *, tm):
    C = DN_CHUNK
    W = DN_QK_WIDTH
    it = pl.program_id(0)

    @pl.when(it == 0)
    def _():
        ext_ref[0:SUBLANES, :] = jnp.zeros((SUBLANES, 3 * W), F32)
        s_ref[...] = jnp.zeros_like(s_ref)
        wba16_ref[...] = jnp.zeros_like(wba16_ref)
        wba16_ref[0:2 * DN_HEADS, :] = wba_ref[...].astype(BF16)

    ext_ref[SUBLANES:SUBLANES + tm, 0:W] = q_ref[...]
    ext_ref[SUBLANES:SUBLANES + tm, W:2 * W] = k_ref[...]
    ext_ref[SUBLANES:SUBLANES + tm, 2 * W:3 * W] = v_ref[...]
    cw = cw_ref[...]
    for cb in range(3 * W // LANES):
        cols = slice(cb * LANES, (cb + 1) * LANES)
        acc = None
        for j in range(DN_CONV):
            off = SUBLANES - (DN_CONV - 1) + j
            term = cw[j:j + 1, cols] * ext_ref[off:off + tm, cols]
            acc = term if acc is None else acc + term
        act_ref[:, cols] = _silu(acc)
    ext_ref[0:SUBLANES, :] = ext_ref[tm:tm + SUBLANES, :]

    ba = _dot_nt(h_ref[...], wba16_ref[...])
    lane = lax.broadcasted_iota(jnp.int32, ba.shape, 1)
    beta = _sigmoid(ba)
    g = -jnp.exp(alog_ref[...]) * _softplus(ba + dtb_ref[...])
    gate_ref[...] = jnp.where(lane < DN_HEADS, beta, g)

    ri = lax.broadcasted_iota(jnp.int32, (C, C), 0)
    ci = lax.broadcasted_iota(jnp.int32, (C, C), 1)
    tril = ri >= ci
    strict = ri > ci
    tril_f = tril.astype(F32)
    triu_f = (ri <= ci).astype(F32)
    eye_f = (ri == ci).astype(F32)
    blk = DN_SOLVE_BLOCK
    same_blk = (ri // blk) == (ci // blk)
    sub_blk = ((ri // blk) == (ci // blk) + 1) & ((ri // (2 * blk)) == (ci // (2 * blk)))
    low_half = (ri >= C // 2) & (ci < C // 2)
    nw = nw_ref[...]
    heads = range(DN_HEADS)

    def bdot(a, b):
        return jnp.dot(a.astype(BF16), b.astype(BF16), preferred_element_type=F32)

    def chunk(c, carry):
        r0 = pl.multiple_of(c * C, C)
        gt = gate_ref[pl.ds(r0, C), :]
        gc = jnp.dot(tril_f, gt, preferred_element_type=F32, precision=HI)
        gc_t = _dot_tn(gt, triu_f, precision=HI)
        q, k, v, kb, vb, gcol, glast, eg, dec = [], [], [], [], [], [], [], [], []
        for h in heads:
            lo = h * DN_DK
            qh = act_ref[pl.ds(r0, C), lo:lo + DN_DK]
            kh = act_ref[pl.ds(r0, C), W + lo:W + lo + DN_DK]
            vh = act_ref[pl.ds(r0, C), 2 * W + lo:2 * W + lo + DN_DV]
            q.append(qh * lax.rsqrt(jnp.sum(qh * qh, axis=-1, keepdims=True) + NORM_EPS) * (DN_DK ** -0.5))
            kh = kh * lax.rsqrt(jnp.sum(kh * kh, axis=-1, keepdims=True) + NORM_EPS)
            k.append(kh)
            bcol = gt[:, h:h + 1]
            gc_h = gc[:, DN_HEADS + h:DN_HEADS + h + 1]
            grow = gc_t[DN_HEADS + h:DN_HEADS + h + 1, :]
            gcol.append(gc_h)
            glast.append(gc_h[C - 1:C, :])
            eg.append(jnp.exp(gc_h))
            dec.append(jnp.where(tril, jnp.exp(jnp.where(tril, gc_h - grow, 0.0)), 0.0))
            kb.append(kh * bcol)
            vb.append(vh * bcol)
            v.append(vh)
        k16 = [k[h].astype(BF16) for h in heads]
        kk = [_dot_nt(kb[h].astype(BF16), k16[h]) for h in heads]
        qk = [_dot_nt(q[h].astype(BF16), k16[h]) for h in heads]
        lmat = [jnp.where(strict, kk[h] * dec[h], 0.0) for h in heads]
        attn = [(qk[h] * dec[h]).astype(BF16) for h in heads]

        def split(x):
            hi = x.astype(BF16)
            return hi, (x - hi.astype(F32)).astype(BF16)

        def dot3(a, b):
            (ah, al), (bh, bl) = a, b
            return (jnp.dot(ah, bh, preferred_element_type=F32)
                    + jnp.dot(al, bh, preferred_element_type=F32)
                    + jnp.dot(ah, bl, preferred_element_type=F32))

        pw = [jnp.where(same_blk, -lmat[h], 0.0) for h in heads]
        tinv = [eye_f + pw[h] for h in heads]
        pws = [split(pw[h]) for h in heads]
        for _ in range(int(math.log2(blk)) - 1):
            pw = [dot3(pws[h], pws[h]) for h in heads]
            pws = [split(pw[h]) for h in heads]
            tinv = [tinv[h] + dot3(split(tinv[h]), pws[h]) for h in heads]
        x1 = [bdot(jnp.where(sub_blk, lmat[h], 0.0), tinv[h]) for h in heads]
        tinv = [tinv[h] - bdot(tinv[h], x1[h]) for h in heads]
        x2 = [bdot(jnp.where(low_half, lmat[h], 0.0), tinv[h]) for h in heads]
        tinv = [tinv[h] - bdot(tinv[h], x2[h]) for h in heads]

        rhs = [jnp.concatenate([vb[h], kb[h] * eg[h]], axis=1) for h in heads]
        sol = [bdot(tinv[h], rhs[h]) for h in heads]
        s = [s_ref[h] for h in heads]
        s16 = [s[h].astype(BF16) for h in heads]
        wq = [jnp.concatenate([sol[h][:, DN_DV:], q[h] * eg[h]], axis=0).astype(BF16) for h in heads]
        wqs = [jnp.dot(wq[h], s16[h], preferred_element_type=F32) for h in heads]
        vn16 = [(sol[h][:, :DN_DV] - wqs[h][:C, :]).astype(BF16) for h in heads]
        kd = [(k[h] * jnp.exp(glast[h] - gcol[h])).astype(BF16) for h in heads]
        o = [wqs[h][C:, :] + jnp.dot(attn[h], vn16[h], preferred_element_type=F32) for h in heads]
        for h in heads:
            s_ref[h] = s[h] * jnp.exp(glast[h]) + _dot_tn(kd[h], vn16[h])
        for h in heads:
            lo = h * DN_DV
            zh = z_ref[pl.ds(r0, C), lo:lo + DN_DV]
            oh = o[h]
            on = oh * lax.rsqrt(jnp.mean(oh * oh, axis=-1, keepdims=True) + NORM_EPS) * nw
            o_ref[pl.ds(r0, C), lo:lo + DN_DV] = (on * _silu(zh)).astype(o_ref.dtype)
        return carry

    lax.fori_loop(0, tm // C, chunk, 0, unroll=2)


def deltanet(proj, h, w_in_t, layer, conv_w, a_log, dt_bias, norm_w, tm=256):
    L = proj.shape[0]
    D = h.shape[1]
    W = DN_QK_WIDTH
    pad = jnp.zeros((1, LANES), F32)
    alog_row = pad.at[0, DN_HEADS:2 * DN_HEADS].set(a_log)
    dtb_row = pad.at[0, DN_HEADS:2 * DN_HEADS].set(dt_bias)
    kern = functools.partial(_dn_kernel, tm=tm)
    return pl.pallas_call(
        kern,
        grid=(L // tm,),
        in_specs=[pl.BlockSpec((tm, W), lambda t: (t, 1)),
                  pl.BlockSpec((tm, W), lambda t: (t, 2)),
                  pl.BlockSpec((tm, W), lambda t: (t, 3)),
                  pl.BlockSpec((tm, W), lambda t: (t, 4)),
                  pl.BlockSpec((tm, D), lambda t: (t, 0)),
                  pl.BlockSpec((None, pl.Element(2 * DN_HEADS), pl.Element(D)), lambda t: (layer, OFF_Z, 0)),
                  pl.BlockSpec((DN_CONV, 3 * W), lambda t: (0, 0)),
                  pl.BlockSpec((1, LANES), lambda t: (0, 0)),
                  pl.BlockSpec((1, LANES), lambda t: (0, 0)),
                  pl.BlockSpec((1, DN_DV), lambda t: (0, 0))],
        out_specs=pl.BlockSpec((tm, DN_V_WIDTH), lambda t: (t, 0)),
        out_shape=jax.ShapeDtypeStruct((L, DN_V_WIDTH), BF16),
        scratch_shapes=[pltpu.VMEM((tm + SUBLANES, 3 * W), F32),
                        pltpu.VMEM((tm, 3 * W), F32),
                        pltpu.VMEM((tm, LANES), F32),
                        pltpu.VMEM((DN_HEADS, DN_DK, DN_DV), F32),
                        pltpu.VMEM((LANES, D), BF16)],
        compiler_params=_cparams(("arbitrary",)),
        name="deltanet",
    )(proj, proj, proj, proj, h, w_in_t, conv_w, alog_row, dtb_row, norm_w.reshape(1, DN_DV))


def _merge_kernel(ys_ref, yd_ref, wga_ref, wgb_ref, wdn_ref, gs_ref, gd_ref, o_ref,
                  wga16_ref, wgb16_ref, wdn16_ref):
    @pl.when(pl.program_id(1) == 0)
    def _():
        wga16_ref[...] = wga_ref[...].astype(BF16)
        wgb16_ref[...] = wgb_ref[...].astype(BF16)
        wdn16_ref[...] = wdn_ref[...].astype(BF16)

    ys = ys_ref[...]
    ga = jnp.dot(ys, wga16_ref[...], preferred_element_type=F32)
    gb = jnp.dot(ys, wgb16_ref[...], preferred_element_type=F32)
    dn = jnp.dot(yd_ref[...], wdn16_ref[...], preferred_element_type=F32)
    br_s5 = ga * _sigmoid(gb)
    o_ref[...] = (_sigmoid(gs_ref[...]) * br_s5 + _sigmoid(gd_ref[...]) * dn).astype(o_ref.dtype)


def merge(y_s5, y_dn, glu_w, dn_w, proj, layer, tm=1024, tn=512):
    L = y_s5.shape[0]
    nb = D_MODEL // tn
    gs_blk = OFF_Z // tn
    gd_blk = gs_blk + nb
    return pl.pallas_call(
        _merge_kernel,
        grid=(nb, L // tm),
        in_specs=[pl.BlockSpec((tm, S5_WIDTH), lambda n, m: (m, 0)),
                  pl.BlockSpec((tm, DN_V_WIDTH), lambda n, m: (m, 0)),
                  pl.BlockSpec((None, S5_WIDTH, tn), lambda n, m: (layer, 0, n)),
                  pl.BlockSpec((None, S5_WIDTH, tn), lambda n, m: (layer, 0, n + nb)),
                  pl.BlockSpec((None, DN_V_WIDTH, tn), lambda n, m: (layer, 0, n)),
                  pl.BlockSpec((tm, tn), lambda n, m: (m, n + gs_blk)),
                  pl.BlockSpec((tm, tn), lambda n, m: (m, n + gd_blk))],
        out_specs=pl.BlockSpec((tm, tn), lambda n, m: (m, n)),
        out_shape=jax.ShapeDtypeStruct((L, D_MODEL), BF16),
        scratch_shapes=[pltpu.VMEM((S5_WIDTH, tn), BF16),
                        pltpu.VMEM((S5_WIDTH, tn), BF16),
                        pltpu.VMEM((DN_V_WIDTH, tn), BF16)],
        compiler_params=_cparams(("parallel", "arbitrary")),
        name="merge",
    )(y_s5, y_dn, glu_w, glu_w, dn_w, proj, proj)


def _out_proj_kernel(m_ref, w_ref, x_ref, nw_ref, xo_ref, h_ref, w16_ref):
    @pl.when(pl.program_id(0) == 0)
    def _():
        w16_ref[...] = w_ref[...].astype(BF16)

    xo = x_ref[...] + jnp.dot(m_ref[...], w16_ref[...], preferred_element_type=F32)
    xo_ref[...] = xo
    h_ref[...] = _rms_rows(xo, nw_ref[...]).astype(h_ref.dtype)


def out_proj(merged, w_out, x, norm_w, layer, tm=512):
    L, D = x.shape
    return pl.pallas_call(
        _out_proj_kernel,
        grid=(L // tm,),
        in_specs=[pl.BlockSpec((tm, D), lambda m: (m, 0)),
                  pl.BlockSpec((None, D, D), lambda m: (layer, 0, 0), pipeline_mode=pl.Buffered(1)),
                  pl.BlockSpec((tm, D), lambda m: (m, 0)),
                  pl.BlockSpec((1, D), lambda m: (0, 0))],
        out_specs=[pl.BlockSpec((tm, D), lambda m: (m, 0)),
                   pl.BlockSpec((tm, D), lambda m: (m, 0))],
        out_shape=[jax.ShapeDtypeStruct((L, D), F32),
                   jax.ShapeDtypeStruct((L, D), BF16)],
        scratch_shapes=[pltpu.VMEM((D, D), BF16)],
        compiler_params=_cparams(("arbitrary",)),
        name="out_proj",
    )(merged, w_out, x, norm_w.reshape(1, D))


def _ffn_up_kernel(h_ref, wa_ref, wv_ref, cwa_ref, cwv_ref, o_ref, ext_ref, w16_ref, *, tm, tn):
    @pl.when(pl.program_id(1) == 0)
    def _():
        ext_ref[...] = jnp.zeros_like(ext_ref)
        w16_ref[:, :tn] = wa_ref[...].astype(BF16)
        w16_ref[:, tn:] = wv_ref[...].astype(BF16)

    up = jnp.dot(h_ref[...], w16_ref[...], preferred_element_type=F32)
    cw = jnp.concatenate([cwa_ref[...], cwv_ref[...]], axis=1)
    prev = ext_ref[...]
    r8 = lax.broadcasted_iota(jnp.int32, prev.shape, 0)
    acc = cw[FFN_CONV - 1:FFN_CONV, :] * up
    for back in range(1, FFN_CONV):
        sh = pltpu.roll(up, back, 0)
        head = jnp.where(r8 < back, pltpu.roll(prev, back, 0), sh[:SUBLANES])
        sh = jnp.concatenate([head, sh[SUBLANES:]], axis=0)
        acc = acc + cw[FFN_CONV - 1 - back:FFN_CONV - back, :] * sh
    o_ref[...] = (_silu(acc[:, :tn]) * acc[:, tn:]).astype(o_ref.dtype)
    ext_ref[...] = up[tm - SUBLANES:, :]


def ffn_up_call(h, w_up, conv_w, layer, tm=1024, tn=512):
    L, D = h.shape
    nb = FFN_DIM // tn
    kern = functools.partial(_ffn_up_kernel, tm=tm, tn=tn)
    return pl.pallas_call(
        kern,
        grid=(nb, L // tm),
        in_specs=[pl.BlockSpec((tm, D), lambda n, m: (m, 0)),
                  pl.BlockSpec((None, D, tn), lambda n, m: (layer, 0, n)),
                  pl.BlockSpec((None, D, tn), lambda n, m: (layer, 0, n + nb)),
                  pl.BlockSpec((FFN_CONV, tn), lambda n, m: (0, n)),
                  pl.BlockSpec((FFN_CONV, tn), lambda n, m: (0, n + nb))],
        out_specs=pl.BlockSpec((tm, tn), lambda n, m: (m, n)),
        out_shape=jax.ShapeDtypeStruct((L, FFN_DIM), BF16),
        scratch_shapes=[pltpu.VMEM((SUBLANES, 2 * tn), F32),
                        pltpu.VMEM((D, 2 * tn), BF16)],
        compiler_params=_cparams(("parallel", "arbitrary")),
        name="ffn_up",
    )(h, w_up, w_up, conv_w, conv_w)


def _ffn_down_kernel(a_ref, w_ref, x_ref, nw_ref, *refs, last):
    acc_ref = refs[0]
    k = pl.program_id(1)

    @pl.when(k == 0)
    def _():
        acc_ref[...] = x_ref[...]

    acc_ref[...] += jnp.dot(a_ref[...], w_ref[...], preferred_element_type=F32)

    @pl.when(k == pl.num_programs(1) - 1)
    def _():
        xo = acc_ref[...]
        if last:
            acc_ref[...] = _rms_rows(xo, nw_ref[...])
        else:
            refs[1][...] = _rms_rows(xo, nw_ref[...]).astype(refs[1].dtype)


def ffn_down_call(a, w_down, x, norm_w, layer, last, tm=1024, tk=512):
    L, D = x.shape
    K = a.shape[1]
    row = pl.BlockSpec((tm, D), lambda m, k: (m, 0))
    if last:
        out_specs = [row]
        out_shape = [jax.ShapeDtypeStruct((L, D), F32)]
    else:
        out_specs = [row, row]
        out_shape = [jax.ShapeDtypeStruct((L, D), F32), jax.ShapeDtypeStruct((L, D), BF16)]
    kern = functools.partial(_ffn_down_kernel, last=last)
    return pl.pallas_call(
        kern,
        grid=(L // tm, K // tk),
        in_specs=[pl.BlockSpec((tm, tk), lambda m, k: (m, k)),
                  pl.BlockSpec((None, tk, D), lambda m, k: (layer, k, 0)),
                  row,
                  pl.BlockSpec((1, D), lambda m, k: (0, 0))],
        out_specs=out_specs,
        out_shape=out_shape,
        compiler_params=_cparams(("parallel", "arbitrary")),
        name="ffn_down",
    )(a, w_down, x, norm_w.reshape(1, D))


def kernel(x, mix_norm_w, w_in, s5_log_dt, s5_a_re, s5_a_im, s5_b_re, s5_b_im, s5_c_re, s5_c_im, s5_d, s5_glu_w, dn_conv_w, dn_a_log, dn_dt_bias, dn_norm_w, dn_proj_w, w_out, ffn_norm_w, ffn_up, ffn_conv_w, ffn_down, final_norm_w):
    B, L, D = x.shape
    depth = w_in.shape[0]
    assert B == 1 and D == D_MODEL
    xs = x.reshape(L, D)
    h = rmsnorm_cast(xs, mix_norm_w[0])
    abr, abi, bbr, bbi = s5_discretise(s5_log_dt, s5_a_re, s5_a_im, s5_b_re, s5_b_im)
    s5_ar, s5_ai, s5_wb, s5_wc = s5_layout_params(abr, abi, bbr, bbi, s5_c_re, s5_c_im, depth)
    s5_ti = 32
    perm, perm_t = s5_perm(s5_ti)
    w_in_t = jnp.swapaxes(w_in, 1, 2)
    w_down = ffn_down.astype(BF16)
    for l in range(depth):
        proj = in_proj(h, w_in_t, l)

        y_s5 = s5_scan(proj, perm, perm_t, s5_ar, s5_ai, s5_wb, s5_wc, s5_d, l, ti=s5_ti)
        y_dn = deltanet(proj, h, w_in_t, l, dn_conv_w[l], dn_a_log[l], dn_dt_bias[l], dn_norm_w[l])

        merged = merge(y_s5, y_dn, s5_glu_w, dn_proj_w, proj, l)
        xs, h = out_proj(merged, w_out, xs, ffn_norm_w[l], l)
        act = ffn_up_call(h, ffn_up, ffn_conv_w[l], l)
        if l + 1 < depth:
            xs, h = ffn_down_call(act, w_down, xs, mix_norm_w[l + 1], l, last=False)
        else:
            (out,) = ffn_down_call(act, w_down, xs, final_norm_w, l, last=True)
    return out.reshape(B, L, D)
```

```python
import functools
import math

import jax
import jax.numpy as jnp
from jax import lax
from jax.experimental import pallas as pl
from jax.experimental.pallas import tpu as pltpu

F32 = jnp.float32
BF16 = jnp.bfloat16

D_MODEL = 2048
S5_WIDTH = 1024
S5_GROUP = 16
S5_GROUPS = 64
S5_STATE = 64
DN_HEADS = 8
DN_DK = 128
DN_DV = 128
DN_QK_WIDTH = DN_HEADS * DN_DK
DN_V_WIDTH = DN_HEADS * DN_DV
DN_CONV = 4
DN_CHUNK = 64
DN_SOLVE_BLOCK = 16
FFN_DIM = 5632
FFN_CONV = 3
NORM_EPS = 1e-6

OFF_U = S5_WIDTH
OFF_QKV = OFF_U + 2 * DN_QK_WIDTH + DN_V_WIDTH
OFF_Z = OFF_QKV + DN_V_WIDTH
OFF_BETA = OFF_Z + DN_HEADS
OFF_ALPHA = OFF_BETA + DN_HEADS
OFF_GS = OFF_ALPHA + D_MODEL
N_IN = OFF_GS + D_MODEL
N_MAIN = N_IN - 2 * DN_HEADS

LANES = 128
SUBLANES = 8
VMEM_LIMIT = 56 * 1024 * 1024

S5_SEGS = SUBLANES
S5_GB = 8
S5_NBLK = S5_GROUPS // S5_GB
S5_BLK_CH = S5_GB * S5_GROUP
S5_BLK_ST = S5_GB * S5_STATE

HI = lax.Precision.HIGHEST


def _cparams(sem):
    return pltpu.CompilerParams(dimension_semantics=sem, vmem_limit_bytes=VMEM_LIMIT)


def _sigmoid(x):
    return 1.0 / (1.0 + jnp.exp(-x))


def _silu(x):
    hx = 0.5 * x
    return hx + hx * jnp.tanh(hx)


def _softplus(x):
    return jnp.maximum(x, 0.0) + jnp.log1p(jnp.exp(-jnp.abs(x)))


def _rms_rows(x, w):
    ms = jnp.mean(x * x, axis=-1, keepdims=True)
    return x * lax.rsqrt(ms + NORM_EPS) * w


def _rmsnorm_kernel(x_ref, w_ref, o_ref):
    o_ref[...] = _rms_rows(x_ref[...], w_ref[...]).astype(o_ref.dtype)


def rmsnorm_cast(x, w, tm=512):
    L, D = x.shape
    return pl.pallas_call(
        _rmsnorm_kernel,
        grid=(L // tm,),
        in_specs=[pl.BlockSpec((tm, D), lambda m: (m, 0)),
                  pl.BlockSpec((1, D), lambda m: (0, 0))],
        out_specs=pl.BlockSpec((tm, D), lambda m: (m, 0)),
        out_shape=jax.ShapeDtypeStruct((L, D), BF16),
        compiler_params=_cparams(("parallel",)),
        name="rmsnorm",
    )(x, w.reshape(1, D))


def _dot_nt(a, b):
    return lax.dot_general(a, b, (((1,), (1,)), ((), ())), preferred_element_type=F32)


def _in_proj_kernel(h_ref, wt_ref, o_ref, w16_ref):
    @pl.when(pl.program_id(1) == 0)
    def _():
        w16_ref[...] = wt_ref[...].astype(BF16)

    o_ref[...] = _dot_nt(h_ref[...], w16_ref[...])


def in_proj(h, w_in_t, layer, tm=1024, tn=1024):
    L, D = h.shape
    n_mix = OFF_Z // tn

    def w_index(n, m):
        row = n * tn + jnp.where(n >= n_mix, OFF_ALPHA - OFF_Z, 0)
        return layer, pl.multiple_of(row, 2 * DN_HEADS), 0

    return pl.pallas_call(
        _in_proj_kernel,
        grid=(N_MAIN // tn, L // tm),
        in_specs=[pl.BlockSpec((tm, D), lambda n, m: (m, 0)),
                  pl.BlockSpec((None, pl.Element(tn), pl.Element(D)), w_index)],
        out_specs=pl.BlockSpec((tm, tn), lambda n, m: (m, n)),
        out_shape=jax.ShapeDtypeStruct((L, N_MAIN), F32),
        scratch_shapes=[pltpu.VMEM((tn, D), BF16)],
        compiler_params=_cparams(("parallel", "arbitrary")),
        name="in_proj",
    )(h, w_in_t)


def _s5_disc_kernel(ldt_ref, are_ref, aim_ref, bre_ref, bim_ref,
                    abr_ref, abi_ref, bbr_ref, bbi_ref):
    lr = are_ref[...]
    li = aim_ref[...]
    dt = jnp.exp(ldt_ref[...])
    mag = jnp.exp(lr * dt)
    abar_re = mag * jnp.cos(li * dt)
    abar_im = mag * jnp.sin(li * dt)
    den = lr * lr + li * li
    nr = abar_re - 1.0
    ni = abar_im
    coef_re = (nr * lr + ni * li) / den
    coef_im = (ni * lr - nr * li) / den
    br = bre_ref[...]
    bi = bim_ref[...]
    abr_ref[...] = abar_re
    abi_ref[...] = abar_im
    bbr_ref[...] = coef_re * br - coef_im * bi
    bbi_ref[...] = coef_re * bi + coef_im * br


def s5_discretise(log_dt, a_re, a_im, b_re, b_im):
    P, HG = S5_STATE, S5_GROUP
    dg = log_dt.size
    bt_re = jnp.transpose(b_re.reshape(dg, P, HG), (0, 2, 1))
    bt_im = jnp.transpose(b_im.reshape(dg, P, HG), (0, 2, 1))
    return pl.pallas_call(
        _s5_disc_kernel,
        out_shape=[jax.ShapeDtypeStruct((dg, 1, P), F32),
                   jax.ShapeDtypeStruct((dg, 1, P), F32),
                   jax.ShapeDtypeStruct((dg, HG, P), F32),
                   jax.ShapeDtypeStruct((dg, HG, P), F32)],
        name="s5_disc",
    )(jnp.broadcast_to(log_dt.reshape(dg, 1, 1), (dg, 1, P)), a_re.reshape(dg, 1, P), a_im.reshape(dg, 1, P),
      bt_re, bt_im)


def _block_diag(t, rows_per_blk, cols_per_blk):
    tiled = jnp.tile(t, (1,) * (t.ndim - 1) + (S5_GB,))
    r = lax.broadcasted_iota(jnp.int32, tiled.shape, t.ndim - 2) // rows_per_blk
    c = lax.broadcasted_iota(jnp.int32, tiled.shape, t.ndim - 1) // cols_per_blk
    return jnp.where(r == c, tiled, jnp.zeros_like(tiled))


def s5_layout_params(abar_re, abar_im, bbar_re, bbar_im, c_re, c_im, depth):
    P, HG = S5_STATE, S5_GROUP
    ns = S5_GROUPS * P
    a_r = jnp.broadcast_to(abar_re.reshape(depth, 1, ns), (depth, S5_SEGS, ns))
    a_i = jnp.broadcast_to(abar_im.reshape(depth, 1, ns), (depth, S5_SEGS, ns))
    bb_re = bbar_re.reshape(depth, S5_NBLK, S5_BLK_CH, P)
    bb_im = bbar_im.reshape(depth, S5_NBLK, S5_BLK_CH, P)
    wb = jnp.concatenate([_block_diag(bb_re, HG, P), _block_diag(bb_im, HG, P)], axis=-1).astype(BF16)
    ct_re = jnp.transpose(c_re, (0, 1, 3, 2)).reshape(depth, S5_NBLK, S5_BLK_ST, HG)
    ct_im = jnp.transpose(c_im, (0, 1, 3, 2)).reshape(depth, S5_NBLK, S5_BLK_ST, HG)
    wc = jnp.concatenate([_block_diag(ct_re, P, HG), -_block_diag(ct_im, P, HG)], axis=-2).astype(BF16)
    return a_r, a_i, wb, wc


def _s5_kernel(u_ref, p_ref, pt_ref, wb_ref, ar_ref, ai_ref, wc_ref, d_ref, y_ref,
               bu_ref, x_ref, sr_ref, si_ref, *, ti, seg_len):
    R = S5_SEGS * ti
    NS = S5_BLK_ST
    ph = pl.program_id(0)
    it = pl.program_id(1)

    @pl.when(jnp.logical_and(ph == 0, it == 0))
    def _():
        sr_ref[...] = jnp.zeros_like(sr_ref)
        si_ref[...] = jnp.zeros_like(si_ref)

    @pl.when(jnp.logical_and(ph == 1, it == 0))
    def _():
        pr, pi = ar_ref[...], ai_ref[...]
        for _ in range(int(math.log2(seg_len))):
            pr, pi = pr * pr - pi * pi, 2.0 * pr * pi
        er, ei = sr_ref[...], si_ref[...]
        row = lax.broadcasted_iota(jnp.int32, er.shape, 0)
        zr = jnp.zeros_like(er)
        zi = jnp.zeros_like(ei)
        for _ in range(S5_SEGS - 1):
            nr = pr * zr - pi * zi + er
            ni = pr * zi + pi * zr + ei
            zr = jnp.where(row >= 1, pltpu.roll(nr, 1, 0), 0.0)
            zi = jnp.where(row >= 1, pltpu.roll(ni, 1, 0), 0.0)
        sr_ref[...] = zr
        si_ref[...] = zi

    u_nat = u_ref[...].reshape(R, S5_WIDTH)
    u2 = jnp.dot(p_ref[...], u_nat.astype(BF16), preferred_element_type=F32).astype(BF16)
    for b in range(S5_NBLK):
        bu_ref[:, b * 2 * NS:(b + 1) * 2 * NS] = jnp.dot(
            u2[:, b * S5_BLK_CH:(b + 1) * S5_BLK_CH], wb_ref[b], preferred_element_type=F32)

    def scan(store):
        for b in range(S5_NBLK):
            ar = ar_ref[:, b * NS:(b + 1) * NS]
            ai = ai_ref[:, b * NS:(b + 1) * NS]
            c_re = b * 2 * NS
            c_im = c_re + NS

            def body(i, carry):
                xr, xi = carry
                r0 = pl.multiple_of(i * SUBLANES, SUBLANES)
                br = bu_ref[pl.ds(r0, SUBLANES), c_re:c_re + NS]
                bi = bu_ref[pl.ds(r0, SUBLANES), c_im:c_im + NS]
                nr = ar * xr - ai * xi + br
                ni = ar * xi + ai * xr + bi
                if store:
                    x_ref[pl.ds(r0, SUBLANES), c_re:c_re + NS] = nr
                    x_ref[pl.ds(r0, SUBLANES), c_im:c_im + NS] = ni
                return nr, ni

            xr, xi = lax.fori_loop(
                0, ti, body, (sr_ref[:, b * NS:(b + 1) * NS], si_ref[:, b * NS:(b + 1) * NS]), unroll=4)
            sr_ref[:, b * NS:(b + 1) * NS] = xr
            si_ref[:, b * NS:(b + 1) * NS] = xi

    @pl.when(ph == 0)
    def _():
        scan(False)

    @pl.when(ph == 1)
    def _():
        scan(True)
        ys = []
        for b in range(S5_NBLK):
            xb = x_ref[:, b * 2 * NS:(b + 1) * 2 * NS].astype(BF16)
            ys.append(jnp.dot(xb, wc_ref[b], preferred_element_type=F32))
        y2 = jnp.concatenate(ys, axis=1)
        y_hi = y2.astype(BF16)
        y_lo = (y2 - y_hi.astype(F32)).astype(BF16)
        pt = pt_ref[...]
        y_nat = (jnp.dot(pt, y_hi, preferred_element_type=F32)
                 + jnp.dot(pt, y_lo, preferred_element_type=F32))
        y_nat = y_nat + d_ref[...] * u_nat
        y = 0.5 * y_nat * (1.0 + lax.erf(y_nat * (1.0 / math.sqrt(2.0))))
        y_ref[...] = y.reshape(S5_SEGS, ti, S5_WIDTH).astype(y_ref.dtype)


def s5_perm(ti):
    R = S5_SEGS * ti
    r = lax.broadcasted_iota(jnp.int32, (R, R), 0)
    c = lax.broadcasted_iota(jnp.int32, (R, R), 1)
    perm = (c == (r % S5_SEGS) * ti + r // S5_SEGS).astype(BF16)
    return perm, perm.T


def s5_scan(proj, perm, perm_t, a_r, a_i, wb, wc, d, layer, ti=32):
    L = proj.shape[0]
    seg_len = L // S5_SEGS
    assert seg_len * S5_SEGS == L and seg_len % ti == 0 and (seg_len & (seg_len - 1)) == 0
    R = S5_SEGS * ti
    nt = seg_len // ti
    proj3 = proj.reshape(S5_SEGS, seg_len, proj.shape[1])
    ns = S5_GROUPS * S5_STATE
    kern = functools.partial(_s5_kernel, ti=ti, seg_len=seg_len)
    y = pl.pallas_call(
        kern,
        grid=(2, nt),
        in_specs=[pl.BlockSpec((S5_SEGS, ti, S5_WIDTH), lambda p, t: (0, t, 0)),
                  pl.BlockSpec((R, R), lambda p, t: (0, 0)),
                  pl.BlockSpec((R, R), lambda p, t: (0, 0)),
                  pl.BlockSpec((None, S5_NBLK, S5_BLK_CH, 2 * S5_BLK_ST), lambda p, t: (layer, 0, 0, 0)),
                  pl.BlockSpec((None, S5_SEGS, ns), lambda p, t: (layer, 0, 0)),
                  pl.BlockSpec((None, S5_SEGS, ns), lambda p, t: (layer, 0, 0)),
                  pl.BlockSpec((None, S5_NBLK, 2 * S5_BLK_ST, S5_BLK_CH), lambda p, t: (layer, 0, 0, 0)),
                  pl.BlockSpec((None, 1, S5_WIDTH), lambda p, t: (layer, 0, 0))],
        out_specs=pl.BlockSpec((S5_SEGS, ti, S5_WIDTH), lambda p, t: (0, t * p, 0)),
        out_shape=jax.ShapeDtypeStruct((S5_SEGS, seg_len, S5_WIDTH), BF16),
        scratch_shapes=[pltpu.VMEM((R, 2 * ns), F32),
                        pltpu.VMEM((R, 2 * ns), F32),
                        pltpu.VMEM((S5_SEGS, ns), F32),
                        pltpu.VMEM((S5_SEGS, ns), F32)],
        compiler_params=_cparams(("arbitrary", "arbitrary")),
        name="s5_scan",
    )(proj3, perm, perm_t, wb, a_r, a_i, wc, d.reshape(-1, 1, S5_WIDTH))
    return y.reshape(L, S5_WIDTH)


def _dot_tn(a, b, precision=None):
    return lax.dot_general(a, b, (((0,), (0,)), ((), ())), preferred_element_type=F32,
                           precision=precision)


def _dn_kernel(q_ref, k_ref, v_ref, z_ref, h_ref, wba_ref, cw_ref, alog_ref, dtb_ref, nw_ref, o_ref,
               ext_ref, act_ref, gate_ref, s_ref, wba16_ref, *, tm):
    C = DN_CHUNK
    W = DN_QK_WIDTH
    it = pl.program_id(0)

    @pl.when(it == 0)
    def _():
        ext_ref[0:SUBLANES, :] = jnp.zeros((SUBLANES, 3 * W), F32)
        s_ref[...] = jnp.zeros_like(s_ref)
        wba16_ref[...] = jnp.zeros_like(wba16_ref)
        wba16_ref[0:2 * DN_HEADS, :] = wba_ref[...].astype(BF16)

    ext_ref[SUBLANES:SUBLANES + tm, 0:W] = q_ref[...]
    ext_ref[SUBLANES:SUBLANES + tm, W:2 * W] = k_ref[...]
    ext_ref[SUBLANES:SUBLANES + tm, 2 * W:3 * W] = v_ref[...]
    cw = cw_ref[...]
    for cb in range(3 * W // LANES):
        cols = slice(cb * LANES, (cb + 1) * LANES)
        acc = None
        for j in range(DN_CONV):
            off = SUBLANES - (DN_CONV - 1) + j
            term = cw[j:j + 1, cols] * ext_ref[off:off + tm, cols]
            acc = term if acc is None else acc + term
        act_ref[:, cols] = _silu(acc)
    ext_ref[0:SUBLANES, :] = ext_ref[tm:tm + SUBLANES, :]

    ba = _dot_nt(h_ref[...], wba16_ref[...])
    lane = lax.broadcasted_iota(jnp.int32, ba.shape, 1)
    beta = _sigmoid(ba)
    g = -jnp.exp(alog_ref[...]) * _softplus(ba + dtb_ref[...])
    gate_ref[...] = jnp.where(lane < DN_HEADS, beta, g)

    ri = lax.broadcasted_iota(jnp.int32, (C, 2 * C), 0)
    cj = lax.broadcasted_iota(jnp.int32, (C, 2 * C), 1)
    right = cj >= C
    ci = jnp.where(right, cj - C, cj)
    tril = ri >= ci
    strict = ri > ci
    eye_f = (ri == ci).astype(F32)
    blk = DN_SOLVE_BLOCK
    same_blk = (ri // blk) == (ci // blk)
    sub_blk = ((ri // blk) == (ci // blk) + 1) & ((ri // (2 * blk)) == (ci // (2 * blk)))
    low_half = (ri >= C // 2) & (ci < C // 2)
    right_row = right[0:1, :]
    rs = lax.broadcasted_iota(jnp.int32, (C, C), 0)
    cs = lax.broadcasted_iota(jnp.int32, (C, C), 1)
    tril_f = (rs >= cs).astype(F32)
    triu2_f = (ri <= ci).astype(F32)
    nw = nw_ref[...]
    heads = range(DN_HEADS)
    pairs = range(DN_HEADS // 2)

    def bd(x):
        z = jnp.zeros_like(x)
        return jnp.concatenate([jnp.where(right, z, x), jnp.where(right, x, z)], axis=0)

    def bd_rows(a, b):
        za = jnp.zeros_like(a)
        return jnp.concatenate([jnp.concatenate([a, za], axis=1), jnp.concatenate([za, b], axis=1)], axis=0)

    def split(x):
        hi = x.astype(BF16)
        return hi, (x - hi.astype(F32)).astype(BF16)

    def dot3(a, b):
        (ah, al), (bh, bl) = a, b
        bdh = bd(bh)
        return (jnp.dot(ah, bdh, preferred_element_type=F32)
                + jnp.dot(al, bdh, preferred_element_type=F32)
                + jnp.dot(ah, bd(bl), preferred_element_type=F32))

    def bdot(a, b):
        return jnp.dot(a.astype(BF16), bd(b.astype(BF16)), preferred_element_type=F32)

    chunks = range(tm // C)
    hu = [(c, h) for c in chunks for h in heads]
    pu = [(c, p) for c in chunks for p in pairs]
    gt = {c: gate_ref[c * C:(c + 1) * C, :] for c in chunks}
    gc = {c: jnp.dot(tril_f, gt[c], preferred_element_type=F32, precision=HI) for c in chunks}
    gc_t = {c: _dot_tn(gt[c], triu2_f, precision=HI) for c in chunks}
    q, k, kb, vb, gcol, glast, eg = {}, {}, {}, {}, {}, {}, {}
    for c, h in hu:
        rows = slice(c * C, (c + 1) * C)
        lo = h * DN_DK
        qh = act_ref[rows, lo:lo + DN_DK]
        kh = act_ref[rows, W + lo:W + lo + DN_DK]
        vh = act_ref[rows, 2 * W + lo:2 * W + lo + DN_DV]
        q[c, h] = qh * lax.rsqrt(jnp.sum(qh * qh, axis=-1, keepdims=True) + NORM_EPS) * (DN_DK ** -0.5)
        kh = kh * lax.rsqrt(jnp.sum(kh * kh, axis=-1, keepdims=True) + NORM_EPS)
        k[c, h] = kh
        bcol = gt[c][:, h:h + 1]
        gc_h = gc[c][:, DN_HEADS + h:DN_HEADS + h + 1]
        gcol[c, h] = gc_h
        glast[c, h] = gc_h[C - 1:C, :]
        eg[c, h] = jnp.exp(gc_h)
        kb[c, h] = kh * bcol
        vb[c, h] = vh * bcol
    k16 = {u: k[u].astype(BF16) for u in hu}
    kbd = {(c, p): bd_rows(k16[c, 2 * p], k16[c, 2 * p + 1]) for c, p in pu}
    kb2 = {(c, p): jnp.concatenate([kb[c, 2 * p], kb[c, 2 * p + 1]], axis=1).astype(BF16) for c, p in pu}
    q2 = {(c, p): jnp.concatenate([q[c, 2 * p], q[c, 2 * p + 1]], axis=1).astype(BF16) for c, p in pu}
    kk = {u: _dot_nt(kb2[u], kbd[u]) for u in pu}
    qk = {u: _dot_nt(q2[u], kbd[u]) for u in pu}
    dec = {}
    for c, p in pu:
        a, b = 2 * p, 2 * p + 1
        gcol2 = jnp.where(right, gcol[c, b], gcol[c, a])
        grow2 = jnp.where(right_row, gc_t[c][DN_HEADS + b:DN_HEADS + b + 1, :],
                          gc_t[c][DN_HEADS + a:DN_HEADS + a + 1, :])
        dec[c, p] = jnp.where(tril, jnp.exp(jnp.where(tril, gcol2 - grow2, 0.0)), 0.0)
    lmat = {u: jnp.where(strict, kk[u] * dec[u], 0.0) for u in pu}
    attn = {u: (qk[u] * dec[u]).astype(BF16) for u in pu}

    pw = {u: jnp.where(same_blk, -lmat[u], 0.0) for u in pu}
    tinv = {u: eye_f + pw[u] for u in pu}
    pws = {u: split(pw[u]) for u in pu}
    for _ in range(int(math.log2(blk)) - 1):
        pw = {u: dot3(pws[u], pws[u]) for u in pu}
        pws = {u: split(pw[u]) for u in pu}
        tinv = {u: tinv[u] + dot3(split(tinv[u]), pws[u]) for u in pu}
    x1 = {u: bdot(jnp.where(sub_blk, lmat[u], 0.0), tinv[u]) for u in pu}
    tinv = {u: tinv[u] - bdot(tinv[u], x1[u]) for u in pu}
    x2 = {u: bdot(jnp.where(low_half, lmat[u], 0.0), tinv[u]) for u in pu}
    tinv = {u: tinv[u] - bdot(tinv[u], x2[u]) for u in pu}

    rhs = {u: jnp.concatenate([vb[u], kb[u] * eg[u]], axis=1).astype(BF16) for u in hu}
    sol2 = {(c, p): jnp.dot(tinv[c, p].astype(BF16), bd_rows(rhs[c, 2 * p], rhs[c, 2 * p + 1]),
                            preferred_element_type=F32) for c, p in pu}
    qe = {u: q[u] * eg[u] for u in hu}
    kd = {u: (k[u] * jnp.exp(glast[u] - gcol[u])).astype(BF16) for u in hu}

    s = [s_ref[h] for h in heads]
    for c in chunks:
        rows = slice(c * C, (c + 1) * C)
        u_c = [sol2[c, h // 2][:, (h % 2) * 2 * DN_DV:(h % 2) * 2 * DN_DV + DN_DV] for h in heads]
        w_c = [sol2[c, h // 2][:, (h % 2) * 2 * DN_DV + DN_DV:(h % 2 + 1) * 2 * DN_DV] for h in heads]
        s16 = [s[h].astype(BF16) for h in heads]
        wq = [jnp.concatenate([w_c[h], qe[c, h]], axis=0).astype(BF16) for h in heads]
        wqs = [jnp.dot(wq[h], s16[h], preferred_element_type=F32) for h in heads]
        vn16 = [(u_c[h] - wqs[h][:C, :]).astype(BF16) for h in heads]
        av2 = [jnp.dot(attn[c, p], bd_rows(vn16[2 * p], vn16[2 * p + 1]), preferred_element_type=F32)
               for p in pairs]
        s = [s[h] * jnp.exp(glast[c, h]) + _dot_tn(kd[c, h], vn16[h]) for h in heads]
        for h in heads:
            lo = h * DN_DV
            zh = z_ref[rows, lo:lo + DN_DV]
            oh = wqs[h][C:, :] + av2[h // 2][:, (h % 2) * DN_DV:(h % 2 + 1) * DN_DV]
            on = oh * lax.rsqrt(jnp.mean(oh * oh, axis=-1, keepdims=True) + NORM_EPS) * nw
            o_ref[rows, lo:lo + DN_DV] = (on * _silu(zh)).astype(o_ref.dtype)
    for h in heads:
        s_ref[h] = s[h]


def deltanet(proj, h, w_in_t, layer, conv_w, a_log, dt_bias, norm_w, tm=256):
    L = proj.shape[0]
    D = h.shape[1]
    W = DN_QK_WIDTH
    pad = jnp.zeros((1, LANES), F32)
    alog_row = pad.at[0, DN_HEADS:2 * DN_HEADS].set(a_log)
    dtb_row = pad.at[0, DN_HEADS:2 * DN_HEADS].set(dt_bias)
    kern = functools.partial(_dn_kernel, tm=tm)
    return pl.pallas_call(
        kern,
        grid=(L // tm,),
        in_specs=[pl.BlockSpec((tm, W), lambda t: (t, 1)),
                  pl.BlockSpec((tm, W), lambda t: (t, 2)),
                  pl.BlockSpec((tm, W), lambda t: (t, 3)),
                  pl.BlockSpec((tm, W), lambda t: (t, 4)),
                  pl.BlockSpec((tm, D), lambda t: (t, 0)),
                  pl.BlockSpec((None, pl.Element(2 * DN_HEADS), pl.Element(D)), lambda t: (layer, OFF_Z, 0)),
                  pl.BlockSpec((DN_CONV, 3 * W), lambda t: (0, 0)),
                  pl.BlockSpec((1, LANES), lambda t: (0, 0)),
                  pl.BlockSpec((1, LANES), lambda t: (0, 0)),
                  pl.BlockSpec((1, DN_DV), lambda t: (0, 0))],
        out_specs=pl.BlockSpec((tm, DN_V_WIDTH), lambda t: (t, 0)),
        out_shape=jax.ShapeDtypeStruct((L, DN_V_WIDTH), BF16),
        scratch_shapes=[pltpu.VMEM((tm + SUBLANES, 3 * W), F32),
                        pltpu.VMEM((tm, 3 * W), F32),
                        pltpu.VMEM((tm, LANES), F32),
                        pltpu.VMEM((DN_HEADS, DN_DK, DN_DV), F32),
                        pltpu.VMEM((LANES, D), BF16)],
        compiler_params=_cparams(("arbitrary",)),
        name="deltanet",
    )(proj, proj, proj, proj, h, w_in_t, conv_w, alog_row, dtb_row, norm_w.reshape(1, DN_DV))


def _merge_kernel(ys_ref, yd_ref, wga_ref, wgb_ref, wdn_ref, gs_ref, gd_ref, o_ref,
                  wga16_ref, wgb16_ref, wdn16_ref):
    @pl.when(pl.program_id(1) == 0)
    def _():
        wga16_ref[...] = wga_ref[...].astype(BF16)
        wgb16_ref[...] = wgb_ref[...].astype(BF16)
        wdn16_ref[...] = wdn_ref[...].astype(BF16)

    ys = ys_ref[...]
    ga = jnp.dot(ys, wga16_ref[...], preferred_element_type=F32)
    gb = jnp.dot(ys, wgb16_ref[...], preferred_element_type=F32)
    dn = jnp.dot(yd_ref[...], wdn16_ref[...], preferred_element_type=F32)
    br_s5 = ga * _sigmoid(gb)
    o_ref[...] = (_sigmoid(gs_ref[...]) * br_s5 + _sigmoid(gd_ref[...]) * dn).astype(o_ref.dtype)


def merge(y_s5, y_dn, glu_w, dn_w, proj, layer, tm=1024, tn=512):
    L = y_s5.shape[0]
    nb = D_MODEL // tn
    gs_blk = OFF_Z // tn
    gd_blk = gs_blk + nb
    return pl.pallas_call(
        _merge_kernel,
        grid=(nb, L // tm),
        in_specs=[pl.BlockSpec((tm, S5_WIDTH), lambda n, m: (m, 0)),
                  pl.BlockSpec((tm, DN_V_WIDTH), lambda n, m: (m, 0)),
                  pl.BlockSpec((None, S5_WIDTH, tn), lambda n, m: (layer, 0, n)),
                  pl.BlockSpec((None, S5_WIDTH, tn), lambda n, m: (layer, 0, n + nb)),
                  pl.BlockSpec((None, DN_V_WIDTH, tn), lambda n, m: (layer, 0, n)),
                  pl.BlockSpec((tm, tn), lambda n, m: (m, n + gs_blk)),
                  pl.BlockSpec((tm, tn), lambda n, m: (m, n + gd_blk))],
        out_specs=pl.BlockSpec((tm, tn), lambda n, m: (m, n)),
        out_shape=jax.ShapeDtypeStruct((L, D_MODEL), BF16),
        scratch_shapes=[pltpu.VMEM((S5_WIDTH, tn), BF16),
                        pltpu.VMEM((S5_WIDTH, tn), BF16),
                        pltpu.VMEM((DN_V_WIDTH, tn), BF16)],
        compiler_params=_cparams(("parallel", "arbitrary")),
        name="merge",
    )(y_s5, y_dn, glu_w, glu_w, dn_w, proj, proj)


def _out_proj_kernel(m_ref, w_ref, x_ref, nw_ref, xo_ref, h_ref, w16_ref):
    @pl.when(pl.program_id(0) == 0)
    def _():
        w16_ref[...] = w_ref[...].astype(BF16)

    xo = x_ref[...] + jnp.dot(m_ref[...], w16_ref[...], preferred_element_type=F32)
    xo_ref[...] = xo
    h_ref[...] = _rms_rows(xo, nw_ref[...]).astype(h_ref.dtype)


def out_proj(merged, w_out, x, norm_w, layer, tm=512):
    L, D = x.shape
    return pl.pallas_call(
        _out_proj_kernel,
        grid=(L // tm,),
        in_specs=[pl.BlockSpec((tm, D), lambda m: (m, 0)),
                  pl.BlockSpec((None, D, D), lambda m: (layer, 0, 0), pipeline_mode=pl.Buffered(1)),
                  pl.BlockSpec((tm, D), lambda m: (m, 0)),
                  pl.BlockSpec((1, D), lambda m: (0, 0))],
        out_specs=[pl.BlockSpec((tm, D), lambda m: (m, 0)),
                   pl.BlockSpec((tm, D), lambda m: (m, 0))],
        out_shape=[jax.ShapeDtypeStruct((L, D), F32),
                   jax.ShapeDtypeStruct((L, D), BF16)],
        scratch_shapes=[pltpu.VMEM((D, D), BF16)],
        compiler_params=_cparams(("arbitrary",)),
        name="out_proj",
    )(merged, w_out, x, norm_w.reshape(1, D))


def _ffn_up_kernel(h_ref, wa_ref, wv_ref, cwa_ref, cwv_ref, o_ref, ext_ref, w16_ref, *, tm, tn):
    @pl.when(pl.program_id(1) == 0)
    def _():
        ext_ref[...] = jnp.zeros_like(ext_ref)
        w16_ref[:, :tn] = wa_ref[...].astype(BF16)
        w16_ref[:, tn:] = wv_ref[...].astype(BF16)

    up = jnp.dot(h_ref[...], w16_ref[...], preferred_element_type=F32)
    cw = jnp.concatenate([cwa_ref[...], cwv_ref[...]], axis=1)
    prev = ext_ref[...]
    r8 = lax.broadcasted_iota(jnp.int32, prev.shape, 0)
    acc = cw[FFN_CONV - 1:FFN_CONV, :] * up
    for back in range(1, FFN_CONV):
        sh = pltpu.roll(up, back, 0)
        head = jnp.where(r8 < back, pltpu.roll(prev, back, 0), sh[:SUBLANES])
        sh = jnp.concatenate([head, sh[SUBLANES:]], axis=0)
        acc = acc + cw[FFN_CONV - 1 - back:FFN_CONV - back, :] * sh
    o_ref[...] = (_silu(acc[:, :tn]) * acc[:, tn:]).astype(o_ref.dtype)
    ext_ref[...] = up[tm - SUBLANES:, :]


def ffn_up_call(h, w_up, conv_w, layer, tm=1024, tn=512):
    L, D = h.shape
    nb = FFN_DIM // tn
    kern = functools.partial(_ffn_up_kernel, tm=tm, tn=tn)
    return pl.pallas_call(
        kern,
        grid=(nb, L // tm),
        in_specs=[pl.BlockSpec((tm, D), lambda n, m: (m, 0)),
                  pl.BlockSpec((None, D, tn), lambda n, m: (layer, 0, n)),
                  pl.BlockSpec((None, D, tn), lambda n, m: (layer, 0, n + nb)),
                  pl.BlockSpec((FFN_CONV, tn), lambda n, m: (0, n)),
                  pl.BlockSpec((FFN_CONV, tn), lambda n, m: (0, n + nb))],
        out_specs=pl.BlockSpec((tm, tn), lambda n, m: (m, n)),
        out_shape=jax.ShapeDtypeStruct((L, FFN_DIM), BF16),
        scratch_shapes=[pltpu.VMEM((SUBLANES, 2 * tn), F32),
                        pltpu.VMEM((D, 2 * tn), BF16)],
        compiler_params=_cparams(("parallel", "arbitrary")),
        name="ffn_up",
    )(h, w_up, w_up, conv_w, conv_w)


def _ffn_down_kernel(a_ref, w_ref, x_ref, nw_ref, *refs, last):
    acc_ref = refs[0]
    k = pl.program_id(1)

    @pl.when(k == 0)
    def _():
        acc_ref[...] = x_ref[...]

    acc_ref[...] += jnp.dot(a_ref[...], w_ref[...], preferred_element_type=F32)

    @pl.when(k == pl.num_programs(1) - 1)
    def _():
        xo = acc_ref[...]
        if last:
            acc_ref[...] = _rms_rows(xo, nw_ref[...])
        else:
            refs[1][...] = _rms_rows(xo, nw_ref[...]).astype(refs[1].dtype)


def ffn_down_call(a, w_down, x, norm_w, layer, last, tm=1024, tk=512):
    L, D = x.shape
    K = a.shape[1]
    row = pl.BlockSpec((tm, D), lambda m, k: (m, 0))
    if last:
        out_specs = [row]
        out_shape = [jax.ShapeDtypeStruct((L, D), F32)]
    else:
        out_specs = [row, row]
        out_shape = [jax.ShapeDtypeStruct((L, D), F32), jax.ShapeDtypeStruct((L, D), BF16)]
    kern = functools.partial(_ffn_down_kernel, last=last)
    return pl.pallas_call(
        kern,
        grid=(L // tm, K // tk),
        in_specs=[pl.BlockSpec((tm, tk), lambda m, k: (m, k)),
                  pl.BlockSpec((None, tk, D), lambda m, k: (layer, k, 0)),
                  row,
                  pl.BlockSpec((1, D), lambda m, k: (0, 0))],
        out_specs=out_specs,
        out_shape=out_shape,
        compiler_params=_cparams(("parallel", "arbitrary")),
        name="ffn_down",
    )(a, w_down, x, norm_w.reshape(1, D))


def kernel(x, mix_norm_w, w_in, s5_log_dt, s5_a_re, s5_a_im, s5_b_re, s5_b_im, s5_c_re, s5_c_im, s5_d, s5_glu_w, dn_conv_w, dn_a_log, dn_dt_bias, dn_norm_w, dn_proj_w, w_out, ffn_norm_w, ffn_up, ffn_conv_w, ffn_down, final_norm_w):
    B, L, D = x.shape
    depth = w_in.shape[0]
    assert B == 1 and D == D_MODEL
    xs = x.reshape(L, D)
    h = rmsnorm_cast(xs, mix_norm_w[0])
    abr, abi, bbr, bbi = s5_discretise(s5_log_dt, s5_a_re, s5_a_im, s5_b_re, s5_b_im)
    s5_ar, s5_ai, s5_wb, s5_wc = s5_layout_params(abr, abi, bbr, bbi, s5_c_re, s5_c_im, depth)
    s5_ti = 32
    perm, perm_t = s5_perm(s5_ti)
    w_in_t = jnp.swapaxes(w_in, 1, 2)
    w_down = ffn_down.astype(BF16)
    for l in range(depth):
        proj = in_proj(h, w_in_t, l)

        y_s5 = s5_scan(proj, perm, perm_t, s5_ar, s5_ai, s5_wb, s5_wc, s5_d, l, ti=s5_ti)
        y_dn = deltanet(proj, h, w_in_t, l, dn_conv_w[l], dn_a_log[l], dn_dt_bias[l], dn_norm_w[l])

        merged = merge(y_s5, y_dn, s5_glu_w, dn_proj_w, proj, l)
        xs, h = out_proj(merged, w_out, xs, ffn_norm_w[l], l)
        act = ffn_up_call(h, ffn_up, ffn_conv_w[l], l)
        if l + 1 < depth:
            xs, h = ffn_down_call(act, w_down, xs, mix_norm_w[l + 1], l, last=False)
        else:
            (out,) = ffn_down_call(act, w_down, xs, final_norm_w, l, last=True)
    return out.reshape(B, L, D)
```

```python
import functools
import math

import jax
import jax.numpy as jnp
from jax import lax
from jax.experimental import pallas as pl
from jax.experimental.pallas import tpu as pltpu

F32 = jnp.float32
BF16 = jnp.bfloat16

D_MODEL = 2048
S5_WIDTH = 1024
S5_GROUP = 16
S5_GROUPS = 64
S5_STATE = 64
DN_HEADS = 8
DN_DK = 128
DN_DV = 128
DN_QK_WIDTH = DN_HEADS * DN_DK
DN_V_WIDTH = DN_HEADS * DN_DV
DN_CONV = 4
DN_CHUNK = 64
DN_SOLVE_BLOCK = 16
FFN_DIM = 5632
FFN_CONV = 3
NORM_EPS = 1e-6

OFF_U = S5_WIDTH
OFF_QKV = OFF_U + 2 * DN_QK_WIDTH + DN_V_WIDTH
OFF_Z = OFF_QKV + DN_V_WIDTH
OFF_BETA = OFF_Z + DN_HEADS
OFF_ALPHA = OFF_BETA + DN_HEADS
OFF_GS = OFF_ALPHA + D_MODEL
N_IN = OFF_GS + D_MODEL
N_MAIN = N_IN - 2 * DN_HEADS

LANES = 128
SUBLANES = 8
VMEM_LIMIT = 56 * 1024 * 1024

S5_SEGS = SUBLANES
S5_GB = 8
S5_NBLK = S5_GROUPS // S5_GB
S5_BLK_CH = S5_GB * S5_GROUP
S5_BLK_ST = S5_GB * S5_STATE

HI = lax.Precision.HIGHEST


def _cparams(sem):
    return pltpu.CompilerParams(dimension_semantics=sem, vmem_limit_bytes=VMEM_LIMIT)


def _sigmoid(x):
    return 1.0 / (1.0 + jnp.exp(-x))


def _silu(x):
    hx = 0.5 * x
    return hx + hx * jnp.tanh(hx)


def _softplus(x):
    return jnp.maximum(x, 0.0) + jnp.log1p(jnp.exp(-jnp.abs(x)))


def _rms_rows(x, w):
    ms = jnp.mean(x * x, axis=-1, keepdims=True)
    return x * lax.rsqrt(ms + NORM_EPS) * w


def _rmsnorm_kernel(x_ref, w_ref, o_ref):
    o_ref[...] = _rms_rows(x_ref[...], w_ref[...]).astype(o_ref.dtype)


def rmsnorm_cast(x, w, tm=512):
    L, D = x.shape
    return pl.pallas_call(
        _rmsnorm_kernel,
        grid=(L // tm,),
        in_specs=[pl.BlockSpec((tm, D), lambda m: (m, 0)),
                  pl.BlockSpec((1, D), lambda m: (0, 0))],
        out_specs=pl.BlockSpec((tm, D), lambda m: (m, 0)),
        out_shape=jax.ShapeDtypeStruct((L, D), BF16),
        compiler_params=_cparams(("parallel",)),
        name="rmsnorm",
    )(x, w.reshape(1, D))


def _dot_nt(a, b):
    return lax.dot_general(a, b, (((1,), (1,)), ((), ())), preferred_element_type=F32)


def _in_proj_kernel(h_ref, wt_ref, o_ref, w16_ref):
    @pl.when(pl.program_id(1) == 0)
    def _():
        w16_ref[...] = wt_ref[...].astype(BF16)

    o_ref[...] = _dot_nt(h_ref[...], w16_ref[...])


def in_proj(h, w_in_t, layer, tm=1024, tn=1024):
    L, D = h.shape
    n_mix = OFF_Z // tn

    def w_index(n, m):
        row = n * tn + jnp.where(n >= n_mix, OFF_ALPHA - OFF_Z, 0)
        return layer, pl.multiple_of(row, 2 * DN_HEADS), 0

    return pl.pallas_call(
        _in_proj_kernel,
        grid=(N_MAIN // tn, L // tm),
        in_specs=[pl.BlockSpec((tm, D), lambda n, m: (m, 0)),
                  pl.BlockSpec((None, pl.Element(tn), pl.Element(D)), w_index)],
        out_specs=pl.BlockSpec((tm, tn), lambda n, m: (m, n)),
        out_shape=jax.ShapeDtypeStruct((L, N_MAIN), F32),
        scratch_shapes=[pltpu.VMEM((tn, D), BF16)],
        compiler_params=_cparams(("parallel", "arbitrary")),
        name="in_proj",
    )(h, w_in_t)


def _s5_disc_kernel(ldt_ref, are_ref, aim_ref, bre_ref, bim_ref,
                    abr_ref, abi_ref, bbr_ref, bbi_ref):
    lr = are_ref[...]
    li = aim_ref[...]
    dt = jnp.exp(ldt_ref[...])
    mag = jnp.exp(lr * dt)
    abar_re = mag * jnp.cos(li * dt)
    abar_im = mag * jnp.sin(li * dt)
    den = lr * lr + li * li
    nr = abar_re - 1.0
    ni = abar_im
    coef_re = (nr * lr + ni * li) / den
    coef_im = (ni * lr - nr * li) / den
    br = bre_ref[...]
    bi = bim_ref[...]
    abr_ref[...] = abar_re
    abi_ref[...] = abar_im
    bbr_ref[...] = coef_re * br - coef_im * bi
    bbi_ref[...] = coef_re * bi + coef_im * br


def s5_discretise(log_dt, a_re, a_im, b_re, b_im):
    P, HG = S5_STATE, S5_GROUP
    dg = log_dt.size
    bt_re = jnp.transpose(b_re.reshape(dg, P, HG), (0, 2, 1))
    bt_im = jnp.transpose(b_im.reshape(dg, P, HG), (0, 2, 1))
    return pl.pallas_call(
        _s5_disc_kernel,
        out_shape=[jax.ShapeDtypeStruct((dg, 1, P), F32),
                   jax.ShapeDtypeStruct((dg, 1, P), F32),
                   jax.ShapeDtypeStruct((dg, HG, P), F32),
                   jax.ShapeDtypeStruct((dg, HG, P), F32)],
        name="s5_disc",
    )(jnp.broadcast_to(log_dt.reshape(dg, 1, 1), (dg, 1, P)), a_re.reshape(dg, 1, P), a_im.reshape(dg, 1, P),
      bt_re, bt_im)


def _block_diag(t, rows_per_blk, cols_per_blk):
    tiled = jnp.tile(t, (1,) * (t.ndim - 1) + (S5_GB,))
    r = lax.broadcasted_iota(jnp.int32, tiled.shape, t.ndim - 2) // rows_per_blk
    c = lax.broadcasted_iota(jnp.int32, tiled.shape, t.ndim - 1) // cols_per_blk
    return jnp.where(r == c, tiled, jnp.zeros_like(tiled))


def s5_layout_params(abar_re, abar_im, bbar_re, bbar_im, c_re, c_im, depth):
    P, HG = S5_STATE, S5_GROUP
    ns = S5_GROUPS * P
    a_r = jnp.broadcast_to(abar_re.reshape(depth, 1, ns), (depth, S5_SEGS, ns))
    a_i = jnp.broadcast_to(abar_im.reshape(depth, 1, ns), (depth, S5_SEGS, ns))
    bb_re = bbar_re.reshape(depth, S5_NBLK, S5_BLK_CH, P)
    bb_im = bbar_im.reshape(depth, S5_NBLK, S5_BLK_CH, P)
    wb = jnp.concatenate([_block_diag(bb_re, HG, P), _block_diag(bb_im, HG, P)], axis=-1).astype(BF16)
    ct_re = jnp.transpose(c_re, (0, 1, 3, 2)).reshape(depth, S5_NBLK, S5_BLK_ST, HG)
    ct_im = jnp.transpose(c_im, (0, 1, 3, 2)).reshape(depth, S5_NBLK, S5_BLK_ST, HG)
    wc = jnp.concatenate([_block_diag(ct_re, P, HG), -_block_diag(ct_im, P, HG)], axis=-2).astype(BF16)
    return a_r, a_i, wb, wc


def _s5_kernel(u_ref, p_ref, pt_ref, wb_ref, ar_ref, ai_ref, wc_ref, d_ref, y_ref,
               bu_ref, x_ref, sr_ref, si_ref, *, ti, seg_len):
    R = S5_SEGS * ti
    NS = S5_BLK_ST
    ph = pl.program_id(0)
    it = pl.program_id(1)

    @pl.when(jnp.logical_and(ph == 0, it == 0))
    def _():
        sr_ref[...] = jnp.zeros_like(sr_ref)
        si_ref[...] = jnp.zeros_like(si_ref)

    @pl.when(jnp.logical_and(ph == 1, it == 0))
    def _():
        pr, pi = ar_ref[...], ai_ref[...]
        for _ in range(int(math.log2(seg_len))):
            pr, pi = pr * pr - pi * pi, 2.0 * pr * pi
        er, ei = sr_ref[...], si_ref[...]
        row = lax.broadcasted_iota(jnp.int32, er.shape, 0)
        zr = jnp.zeros_like(er)
        zi = jnp.zeros_like(ei)
        for _ in range(S5_SEGS - 1):
            nr = pr * zr - pi * zi + er
            ni = pr * zi + pi * zr + ei
            zr = jnp.where(row >= 1, pltpu.roll(nr, 1, 0), 0.0)
            zi = jnp.where(row >= 1, pltpu.roll(ni, 1, 0), 0.0)
        sr_ref[...] = zr
        si_ref[...] = zi

    u_nat = u_ref[...].reshape(R, S5_WIDTH)
    u2 = jnp.dot(p_ref[...], u_nat.astype(BF16), preferred_element_type=F32).astype(BF16)
    for b in range(S5_NBLK):
        bu_ref[:, b * 2 * NS:(b + 1) * 2 * NS] = jnp.dot(
            u2[:, b * S5_BLK_CH:(b + 1) * S5_BLK_CH], wb_ref[b], preferred_element_type=F32)

    def scan(store):
        for b in range(S5_NBLK):
            ar = ar_ref[:, b * NS:(b + 1) * NS]
            ai = ai_ref[:, b * NS:(b + 1) * NS]
            c_re = b * 2 * NS
            c_im = c_re + NS

            def body(i, carry):
                xr, xi = carry
                r0 = pl.multiple_of(i * SUBLANES, SUBLANES)
                br = bu_ref[pl.ds(r0, SUBLANES), c_re:c_re + NS]
                bi = bu_ref[pl.ds(r0, SUBLANES), c_im:c_im + NS]
                nr = ar * xr - ai * xi + br
                ni = ar * xi + ai * xr + bi
                if store:
                    x_ref[pl.ds(r0, SUBLANES), c_re:c_re + NS] = nr
                    x_ref[pl.ds(r0, SUBLANES), c_im:c_im + NS] = ni
                return nr, ni

            xr, xi = lax.fori_loop(
                0, ti, body, (sr_ref[:, b * NS:(b + 1) * NS], si_ref[:, b * NS:(b + 1) * NS]), unroll=4)
            sr_ref[:, b * NS:(b + 1) * NS] = xr
            si_ref[:, b * NS:(b + 1) * NS] = xi

    @pl.when(ph == 0)
    def _():
        scan(False)

    @pl.when(ph == 1)
    def _():
        scan(True)
        ys = []
        for b in range(S5_NBLK):
            xb = x_ref[:, b * 2 * NS:(b + 1) * 2 * NS].astype(BF16)
            ys.append(jnp.dot(xb, wc_ref[b], preferred_element_type=F32))
        y2 = jnp.concatenate(ys, axis=1)
        y_hi = y2.astype(BF16)
        y_lo = (y2 - y_hi.astype(F32)).astype(BF16)
        pt = pt_ref[...]
        y_nat = (jnp.dot(pt, y_hi, preferred_element_type=F32)
                 + jnp.dot(pt, y_lo, preferred_element_type=F32))
        y_nat = y_nat + d_ref[...] * u_nat
        y = 0.5 * y_nat * (1.0 + lax.erf(y_nat * (1.0 / math.sqrt(2.0))))
        y_ref[...] = y.reshape(S5_SEGS, ti, S5_WIDTH).astype(y_ref.dtype)


def s5_perm(ti):
    R = S5_SEGS * ti
    r = lax.broadcasted_iota(jnp.int32, (R, R), 0)
    c = lax.broadcasted_iota(jnp.int32, (R, R), 1)
    perm = (c == (r % S5_SEGS) * ti + r // S5_SEGS).astype(BF16)
    return perm, perm.T


def s5_scan(proj, perm, perm_t, a_r, a_i, wb, wc, d, layer, ti=32):
    L = proj.shape[0]
    seg_len = L // S5_SEGS
    assert seg_len * S5_SEGS == L and seg_len % ti == 0 and (seg_len & (seg_len - 1)) == 0
    R = S5_SEGS * ti
    nt = seg_len // ti
    proj3 = proj.reshape(S5_SEGS, seg_len, proj.shape[1])
    ns = S5_GROUPS * S5_STATE
    kern = functools.partial(_s5_kernel, ti=ti, seg_len=seg_len)
    y = pl.pallas_call(
        kern,
        grid=(2, nt),
        in_specs=[pl.BlockSpec((S5_SEGS, ti, S5_WIDTH), lambda p, t: (0, t, 0)),
                  pl.BlockSpec((R, R), lambda p, t: (0, 0)),
                  pl.BlockSpec((R, R), lambda p, t: (0, 0)),
                  pl.BlockSpec((None, S5_NBLK, S5_BLK_CH, 2 * S5_BLK_ST), lambda p, t: (layer, 0, 0, 0)),
                  pl.BlockSpec((None, S5_SEGS, ns), lambda p, t: (layer, 0, 0)),
                  pl.BlockSpec((None, S5_SEGS, ns), lambda p, t: (layer, 0, 0)),
                  pl.BlockSpec((None, S5_NBLK, 2 * S5_BLK_ST, S5_BLK_CH), lambda p, t: (layer, 0, 0, 0)),
                  pl.BlockSpec((None, 1, S5_WIDTH), lambda p, t: (layer, 0, 0))],
        out_specs=pl.BlockSpec((S5_SEGS, ti, S5_WIDTH), lambda p, t: (0, t * p, 0)),
        out_shape=jax.ShapeDtypeStruct((S5_SEGS, seg_len, S5_WIDTH), BF16),
        scratch_shapes=[pltpu.VMEM((R, 2 * ns), F32),
                        pltpu.VMEM((R, 2 * ns), F32),
                        pltpu.VMEM((S5_SEGS, ns), F32),
                        pltpu.VMEM((S5_SEGS, ns), F32)],
        compiler_params=_cparams(("arbitrary", "arbitrary")),
        name="s5_scan",
    )(proj3, perm, perm_t, wb, a_r, a_i, wc, d.reshape(-1, 1, S5_WIDTH))
    return y.reshape(L, S5_WIDTH)


def _dot_tn(a, b, precision=None):
    return lax.dot_general(a, b, (((0,), (0,)), ((), ())), preferred_element_type=F32,
                           precision=precision)


def _dn_kernel(q_ref, k_ref, v_ref, z_ref, h_ref, wba_ref, cw_ref, alog_ref, dtb_ref, nw_ref, o_ref,
               ext_ref, act_ref, gate_ref, s_ref, wba16_ref, *, tm):
    C = DN_CHUNK
    W = DN_QK_WIDTH
    it = pl.program_id(0)

    @pl.when(it == 0)
    def _():
        ext_ref[0:SUBLANES, :] = jnp.zeros((SUBLANES, 3 * W), F32)
        s_ref[...] = jnp.zeros_like(s_ref)
        wba16_ref[...] = jnp.zeros_like(wba16_ref)
        wba16_ref[0:2 * DN_HEADS, :] = wba_ref[...].astype(BF16)

    ext_ref[SUBLANES:SUBLANES + tm, 0:W] = q_ref[...]
    ext_ref[SUBLANES:SUBLANES + tm, W:2 * W] = k_ref[...]
    ext_ref[SUBLANES:SUBLANES + tm, 2 * W:3 * W] = v_ref[...]
    cw = cw_ref[...]
    for cb in range(3 * W // LANES):
        cols = slice(cb * LANES, (cb + 1) * LANES)
        x = ext_ref[:, cols]
        acc = cw[DN_CONV - 1:DN_CONV, cols] * x[SUBLANES:]
        for back in range(1, DN_CONV):
            acc = acc + cw[DN_CONV - 1 - back:DN_CONV - back, cols] * pltpu.roll(x, back, 0)[SUBLANES:]
        act_ref[:, cols] = _silu(acc)
    ext_ref[0:SUBLANES, :] = ext_ref[tm:tm + SUBLANES, :]

    ba = _dot_nt(h_ref[...], wba16_ref[...])
    lane = lax.broadcasted_iota(jnp.int32, ba.shape, 1)
    beta = _sigmoid(ba)
    g = -jnp.exp(alog_ref[...]) * _softplus(ba + dtb_ref[...])
    gate_ref[...] = jnp.where(lane < DN_HEADS, beta, g)

    ri = lax.broadcasted_iota(jnp.int32, (C, 2 * C), 0)
    cj = lax.broadcasted_iota(jnp.int32, (C, 2 * C), 1)
    right = cj >= C
    ci = jnp.where(right, cj - C, cj)
    tril = ri >= ci
    strict = ri > ci
    eye_f = (ri == ci).astype(F32)
    blk = DN_SOLVE_BLOCK
    same_blk = (ri // blk) == (ci // blk)
    sub_blk = ((ri // blk) == (ci // blk) + 1) & ((ri // (2 * blk)) == (ci // (2 * blk)))
    low_half = (ri >= C // 2) & (ci < C // 2)
    right_row = right[0:1, :]
    rs = lax.broadcasted_iota(jnp.int32, (C, C), 0)
    cs = lax.broadcasted_iota(jnp.int32, (C, C), 1)
    tril_f = (rs >= cs).astype(F32)
    triu2_f = (ri <= ci).astype(F32)
    nw = nw_ref[...]
    heads = range(DN_HEADS)
    pairs = range(DN_HEADS // 2)

    def bd(x):
        z = jnp.zeros_like(x)
        return jnp.concatenate([jnp.where(right, z, x), jnp.where(right, x, z)], axis=0)

    def bd_rows(a, b):
        za = jnp.zeros_like(a)
        return jnp.concatenate([jnp.concatenate([a, za], axis=1), jnp.concatenate([za, b], axis=1)], axis=0)

    def split(x):
        hi = x.astype(BF16)
        return hi, (x - hi.astype(F32)).astype(BF16)

    def dot3(a, b):
        (ah, al), (bh, bl) = a, b
        bdh = bd(bh)
        return (jnp.dot(ah, bdh, preferred_element_type=F32)
                + jnp.dot(al, bdh, preferred_element_type=F32)
                + jnp.dot(ah, bd(bl), preferred_element_type=F32))

    def bdot(a, b):
        return jnp.dot(a.astype(BF16), bd(b.astype(BF16)), preferred_element_type=F32)

    chunks = range(tm // C)
    hu = [(c, h) for c in chunks for h in heads]
    pu = [(c, p) for c in chunks for p in pairs]
    gt = {c: gate_ref[c * C:(c + 1) * C, :] for c in chunks}
    gc = {c: jnp.dot(tril_f, gt[c], preferred_element_type=F32, precision=HI) for c in chunks}
    gc_t = {c: _dot_tn(gt[c], triu2_f, precision=HI) for c in chunks}
    q, k, kb, vb, gcol, glast, eg = {}, {}, {}, {}, {}, {}, {}
    for c, h in hu:
        rows = slice(c * C, (c + 1) * C)
        lo = h * DN_DK
        qh = act_ref[rows, lo:lo + DN_DK]
        kh = act_ref[rows, W + lo:W + lo + DN_DK]
        vh = act_ref[rows, 2 * W + lo:2 * W + lo + DN_DV]
        q[c, h] = qh * lax.rsqrt(jnp.sum(qh * qh, axis=-1, keepdims=True) + NORM_EPS) * (DN_DK ** -0.5)
        kh = kh * lax.rsqrt(jnp.sum(kh * kh, axis=-1, keepdims=True) + NORM_EPS)
        k[c, h] = kh
        bcol = gt[c][:, h:h + 1]
        gc_h = gc[c][:, DN_HEADS + h:DN_HEADS + h + 1]
        gcol[c, h] = gc_h
        glast[c, h] = gc_h[C - 1:C, :]
        eg[c, h] = jnp.exp(gc_h)
        kb[c, h] = kh * bcol
        vb[c, h] = vh * bcol
    k16 = {u: k[u].astype(BF16) for u in hu}
    kbd = {(c, p): bd_rows(k16[c, 2 * p], k16[c, 2 * p + 1]) for c, p in pu}
    kb2 = {(c, p): jnp.concatenate([kb[c, 2 * p], kb[c, 2 * p + 1]], axis=1).astype(BF16) for c, p in pu}
    q2 = {(c, p): jnp.concatenate([q[c, 2 * p], q[c, 2 * p + 1]], axis=1).astype(BF16) for c, p in pu}
    kk = {u: _dot_nt(kb2[u], kbd[u]) for u in pu}
    qk = {u: _dot_nt(q2[u], kbd[u]) for u in pu}
    dec = {}
    for c, p in pu:
        a, b = 2 * p, 2 * p + 1
        gcol2 = jnp.where(right, gcol[c, b], gcol[c, a])
        grow2 = jnp.where(right_row, gc_t[c][DN_HEADS + b:DN_HEADS + b + 1, :],
                          gc_t[c][DN_HEADS + a:DN_HEADS + a + 1, :])
        dec[c, p] = jnp.where(tril, jnp.exp(jnp.where(tril, gcol2 - grow2, 0.0)), 0.0)
    lmat = {u: jnp.where(strict, kk[u] * dec[u], 0.0) for u in pu}
    attn = {u: (qk[u] * dec[u]).astype(BF16) for u in pu}

    pw = {u: jnp.where(same_blk, -lmat[u], 0.0) for u in pu}
    tinv = {u: eye_f + pw[u] for u in pu}
    pws = {u: split(pw[u]) for u in pu}
    for _ in range(int(math.log2(blk)) - 1):
        pw = {u: dot3(pws[u], pws[u]) for u in pu}
        pws = {u: split(pw[u]) for u in pu}
        tinv = {u: tinv[u] + dot3(split(tinv[u]), pws[u]) for u in pu}
    x1 = {u: bdot(jnp.where(sub_blk, lmat[u], 0.0), tinv[u]) for u in pu}
    tinv = {u: tinv[u] - bdot(tinv[u], x1[u]) for u in pu}
    x2 = {u: bdot(jnp.where(low_half, lmat[u], 0.0), tinv[u]) for u in pu}
    tinv = {u: tinv[u] - bdot(tinv[u], x2[u]) for u in pu}

    rhs = {u: jnp.concatenate([vb[u], kb[u] * eg[u]], axis=1).astype(BF16) for u in hu}
    sol2 = {(c, p): jnp.dot(tinv[c, p].astype(BF16), bd_rows(rhs[c, 2 * p], rhs[c, 2 * p + 1]),
                            preferred_element_type=F32) for c, p in pu}
    qe = {u: q[u] * eg[u] for u in hu}
    kd = {u: (k[u] * jnp.exp(glast[u] - gcol[u])).astype(BF16) for u in hu}

    s = [s_ref[h] for h in heads]
    for c in chunks:
        rows = slice(c * C, (c + 1) * C)
        u_c = [sol2[c, h // 2][:, (h % 2) * 2 * DN_DV:(h % 2) * 2 * DN_DV + DN_DV] for h in heads]
        w_c = [sol2[c, h // 2][:, (h % 2) * 2 * DN_DV + DN_DV:(h % 2 + 1) * 2 * DN_DV] for h in heads]
        s16 = [s[h].astype(BF16) for h in heads]
        wq = [jnp.concatenate([w_c[h], qe[c, h]], axis=0).astype(BF16) for h in heads]
        wqs = [jnp.dot(wq[h], s16[h], preferred_element_type=F32) for h in heads]
        vn16 = [(u_c[h] - wqs[h][:C, :]).astype(BF16) for h in heads]
        av2 = [jnp.dot(attn[c, p], bd_rows(vn16[2 * p], vn16[2 * p + 1]), preferred_element_type=F32)
               for p in pairs]
        s = [s[h] * jnp.exp(glast[c, h]) + _dot_tn(kd[c, h], vn16[h]) for h in heads]
        for h in heads:
            lo = h * DN_DV
            zh = z_ref[rows, lo:lo + DN_DV]
            oh = wqs[h][C:, :] + av2[h // 2][:, (h % 2) * DN_DV:(h % 2 + 1) * DN_DV]
            on = oh * lax.rsqrt(jnp.mean(oh * oh, axis=-1, keepdims=True) + NORM_EPS) * nw
            o_ref[rows, lo:lo + DN_DV] = (on * _silu(zh)).astype(o_ref.dtype)
    for h in heads:
        s_ref[h] = s[h]


def deltanet(proj, h, w_in_t, layer, conv_w, a_log, dt_bias, norm_w, tm=256):
    L = proj.shape[0]
    D = h.shape[1]
    W = DN_QK_WIDTH
    pad = jnp.zeros((1, LANES), F32)
    alog_row = pad.at[0, DN_HEADS:2 * DN_HEADS].set(a_log)
    dtb_row = pad.at[0, DN_HEADS:2 * DN_HEADS].set(dt_bias)
    kern = functools.partial(_dn_kernel, tm=tm)
    return pl.pallas_call(
        kern,
        grid=(L // tm,),
        in_specs=[pl.BlockSpec((tm, W), lambda t: (t, 1)),
                  pl.BlockSpec((tm, W), lambda t: (t, 2)),
                  pl.BlockSpec((tm, W), lambda t: (t, 3)),
                  pl.BlockSpec((tm, W), lambda t: (t, 4)),
                  pl.BlockSpec((tm, D), lambda t: (t, 0)),
                  pl.BlockSpec((None, pl.Element(2 * DN_HEADS), pl.Element(D)), lambda t: (layer, OFF_Z, 0)),
                  pl.BlockSpec((DN_CONV, 3 * W), lambda t: (0, 0)),
                  pl.BlockSpec((1, LANES), lambda t: (0, 0)),
                  pl.BlockSpec((1, LANES), lambda t: (0, 0)),
                  pl.BlockSpec((1, DN_DV), lambda t: (0, 0))],
        out_specs=pl.BlockSpec((tm, DN_V_WIDTH), lambda t: (t, 0)),
        out_shape=jax.ShapeDtypeStruct((L, DN_V_WIDTH), BF16),
        scratch_shapes=[pltpu.VMEM((tm + SUBLANES, 3 * W), F32),
                        pltpu.VMEM((tm, 3 * W), F32),
                        pltpu.VMEM((tm, LANES), F32),
                        pltpu.VMEM((DN_HEADS, DN_DK, DN_DV), F32),
                        pltpu.VMEM((LANES, D), BF16)],
        compiler_params=_cparams(("arbitrary",)),
        name="deltanet",
    )(proj, proj, proj, proj, h, w_in_t, conv_w, alog_row, dtb_row, norm_w.reshape(1, DN_DV))


def _merge_kernel(ys_ref, yd_ref, wga_ref, wgb_ref, wdn_ref, gs_ref, gd_ref, o_ref,
                  wga16_ref, wgb16_ref, wdn16_ref):
    @pl.when(pl.program_id(1) == 0)
    def _():
        wga16_ref[...] = wga_ref[...].astype(BF16)
        wgb16_ref[...] = wgb_ref[...].astype(BF16)
        wdn16_ref[...] = wdn_ref[...].astype(BF16)

    ys = ys_ref[...]
    ga = jnp.dot(ys, wga16_ref[...], preferred_element_type=F32)
    gb = jnp.dot(ys, wgb16_ref[...], preferred_element_type=F32)
    dn = jnp.dot(yd_ref[...], wdn16_ref[...], preferred_element_type=F32)
    br_s5 = ga * _sigmoid(gb)
    o_ref[...] = (_sigmoid(gs_ref[...]) * br_s5 + _sigmoid(gd_ref[...]) * dn).astype(o_ref.dtype)


def merge(y_s5, y_dn, glu_w, dn_w, proj, layer, tm=1024, tn=512):
    L = y_s5.shape[0]
    nb = D_MODEL // tn
    gs_blk = OFF_Z // tn
    gd_blk = gs_blk + nb
    return pl.pallas_call(
        _merge_kernel,
        grid=(nb, L // tm),
        in_specs=[pl.BlockSpec((tm, S5_WIDTH), lambda n, m: (m, 0)),
                  pl.BlockSpec((tm, DN_V_WIDTH), lambda n, m: (m, 0)),
                  pl.BlockSpec((None, S5_WIDTH, tn), lambda n, m: (layer, 0, n)),
                  pl.BlockSpec((None, S5_WIDTH, tn), lambda n, m: (layer, 0, n + nb)),
                  pl.BlockSpec((None, DN_V_WIDTH, tn), lambda n, m: (layer, 0, n)),
                  pl.BlockSpec((tm, tn), lambda n, m: (m, n + gs_blk)),
                  pl.BlockSpec((tm, tn), lambda n, m: (m, n + gd_blk))],
        out_specs=pl.BlockSpec((tm, tn), lambda n, m: (m, n)),
        out_shape=jax.ShapeDtypeStruct((L, D_MODEL), BF16),
        scratch_shapes=[pltpu.VMEM((S5_WIDTH, tn), BF16),
                        pltpu.VMEM((S5_WIDTH, tn), BF16),
                        pltpu.VMEM((DN_V_WIDTH, tn), BF16)],
        compiler_params=_cparams(("parallel", "arbitrary")),
        name="merge",
    )(y_s5, y_dn, glu_w, glu_w, dn_w, proj, proj)


def _out_proj_kernel(m_ref, w_ref, x_ref, nw_ref, xo_ref, h_ref, w16_ref):
    @pl.when(pl.program_id(0) == 0)
    def _():
        w16_ref[...] = w_ref[...].astype(BF16)

    xo = x_ref[...] + jnp.dot(m_ref[...], w16_ref[...], preferred_element_type=F32)
    xo_ref[...] = xo
    h_ref[...] = _rms_rows(xo, nw_ref[...]).astype(h_ref.dtype)


def out_proj(merged, w_out, x, norm_w, layer, tm=512):
    L, D = x.shape
    return pl.pallas_call(
        _out_proj_kernel,
        grid=(L // tm,),
        in_specs=[pl.BlockSpec((tm, D), lambda m: (m, 0)),
                  pl.BlockSpec((None, D, D), lambda m: (layer, 0, 0), pipeline_mode=pl.Buffered(1)),
                  pl.BlockSpec((tm, D), lambda m: (m, 0)),
                  pl.BlockSpec((1, D), lambda m: (0, 0))],
        out_specs=[pl.BlockSpec((tm, D), lambda m: (m, 0)),
                   pl.BlockSpec((tm, D), lambda m: (m, 0))],
        out_shape=[jax.ShapeDtypeStruct((L, D), F32),
                   jax.ShapeDtypeStruct((L, D), BF16)],
        scratch_shapes=[pltpu.VMEM((D, D), BF16)],
        compiler_params=_cparams(("arbitrary",)),
        name="out_proj",
    )(merged, w_out, x, norm_w.reshape(1, D))


def _ffn_up_kernel(h_ref, wa_ref, wv_ref, cwa_ref, cwv_ref, o_ref, ext_ref, w16_ref, *, tm, tn):
    @pl.when(pl.program_id(1) == 0)
    def _():
        ext_ref[...] = jnp.zeros_like(ext_ref)
        w16_ref[:, :tn] = wa_ref[...].astype(BF16)
        w16_ref[:, tn:] = wv_ref[...].astype(BF16)

    up = jnp.dot(h_ref[...], w16_ref[...], preferred_element_type=F32)
    cw = jnp.concatenate([cwa_ref[...], cwv_ref[...]], axis=1)
    prev = ext_ref[...]
    r8 = lax.broadcasted_iota(jnp.int32, prev.shape, 0)
    acc = cw[FFN_CONV - 1:FFN_CONV, :] * up
    for back in range(1, FFN_CONV):
        sh = pltpu.roll(up, back, 0)
        head = jnp.where(r8 < back, pltpu.roll(prev, back, 0), sh[:SUBLANES])
        sh = jnp.concatenate([head, sh[SUBLANES:]], axis=0)
        acc = acc + cw[FFN_CONV - 1 - back:FFN_CONV - back, :] * sh
    o_ref[...] = (_silu(acc[:, :tn]) * acc[:, tn:]).astype(o_ref.dtype)
    ext_ref[...] = up[tm - SUBLANES:, :]


def ffn_up_call(h, w_up, conv_w, layer, tm=1024, tn=512):
    L, D = h.shape
    nb = FFN_DIM // tn
    kern = functools.partial(_ffn_up_kernel, tm=tm, tn=tn)
    return pl.pallas_call(
        kern,
        grid=(nb, L // tm),
        in_specs=[pl.BlockSpec((tm, D), lambda n, m: (m, 0)),
                  pl.BlockSpec((None, D, tn), lambda n, m: (layer, 0, n)),
                  pl.BlockSpec((None, D, tn), lambda n, m: (layer, 0, n + nb)),
                  pl.BlockSpec((FFN_CONV, tn), lambda n, m: (0, n)),
                  pl.BlockSpec((FFN_CONV, tn), lambda n, m: (0, n + nb))],
        out_specs=pl.BlockSpec((tm, tn), lambda n, m: (m, n)),
        out_shape=jax.ShapeDtypeStruct((L, FFN_DIM), BF16),
        scratch_shapes=[pltpu.VMEM((SUBLANES, 2 * tn), F32),
                        pltpu.VMEM((D, 2 * tn), BF16)],
        compiler_params=_cparams(("parallel", "arbitrary")),
        name="ffn_up",
    )(h, w_up, w_up, conv_w, conv_w)


def _ffn_down_kernel(a_ref, w_ref, x_ref, nw_ref, *refs, last):
    acc_ref = refs[0]
    k = pl.program_id(1)

    @pl.when(k == 0)
    def _():
        acc_ref[...] = x_ref[...]

    acc_ref[...] += jnp.dot(a_ref[...], w_ref[...], preferred_element_type=F32)

    @pl.when(k == pl.num_programs(1) - 1)
    def _():
        xo = acc_ref[...]
        if last:
            acc_ref[...] = _rms_rows(xo, nw_ref[...])
        else:
            refs[1][...] = _rms_rows(xo, nw_ref[...]).astype(refs[1].dtype)


def ffn_down_call(a, w_down, x, norm_w, layer, last, tm=1024, tk=512):
    L, D = x.shape
    K = a.shape[1]
    row = pl.BlockSpec((tm, D), lambda m, k: (m, 0))
    if last:
        out_specs = [row]
        out_shape = [jax.ShapeDtypeStruct((L, D), F32)]
    else:
        out_specs = [row, row]
        out_shape = [jax.ShapeDtypeStruct((L, D), F32), jax.ShapeDtypeStruct((L, D), BF16)]
    kern = functools.partial(_ffn_down_kernel, last=last)
    return pl.pallas_call(
        kern,
        grid=(L // tm, K // tk),
        in_specs=[pl.BlockSpec((tm, tk), lambda m, k: (m, k)),
                  pl.BlockSpec((None, tk, D), lambda m, k: (layer, k, 0)),
                  row,
                  pl.BlockSpec((1, D), lambda m, k: (0, 0))],
        out_specs=out_specs,
        out_shape=out_shape,
        compiler_params=_cparams(("parallel", "arbitrary")),
        name="ffn_down",
    )(a, w_down, x, norm_w.reshape(1, D))


def kernel(x, mix_norm_w, w_in, s5_log_dt, s5_a_re, s5_a_im, s5_b_re, s5_b_im, s5_c_re, s5_c_im, s5_d, s5_glu_w, dn_conv_w, dn_a_log, dn_dt_bias, dn_norm_w, dn_proj_w, w_out, ffn_norm_w, ffn_up, ffn_conv_w, ffn_down, final_norm_w):
    B, L, D = x.shape
    depth = w_in.shape[0]
    assert B == 1 and D == D_MODEL
    xs = x.reshape(L, D)
    h = rmsnorm_cast(xs, mix_norm_w[0])
    abr, abi, bbr, bbi = s5_discretise(s5_log_dt, s5_a_re, s5_a_im, s5_b_re, s5_b_im)
    s5_ar, s5_ai, s5_wb, s5_wc = s5_layout_params(abr, abi, bbr, bbi, s5_c_re, s5_c_im, depth)
    s5_ti = 32
    perm, perm_t = s5_perm(s5_ti)
    w_in_t = jnp.swapaxes(w_in, 1, 2)
    w_down = ffn_down.astype(BF16)
    for l in range(depth):
        proj = in_proj(h, w_in_t, l)

        y_s5 = s5_scan(proj, perm, perm_t, s5_ar, s5_ai, s5_wb, s5_wc, s5_d, l, ti=s5_ti)
        y_dn = deltanet(proj, h, w_in_t, l, dn_conv_w[l], dn_a_log[l], dn_dt_bias[l], dn_norm_w[l])

        merged = merge(y_s5, y_dn, s5_glu_w, dn_proj_w, proj, l)
        xs, h = out_proj(merged, w_out, xs, ffn_norm_w[l], l)
        act = ffn_up_call(h, ffn_up, ffn_conv_w[l], l)
        if l + 1 < depth:
            xs, h = ffn_down_call(act, w_down, xs, mix_norm_w[l + 1], l, last=False)
        else:
            (out,) = ffn_down_call(act, w_down, xs, final_norm_w, l, last=True)
    return out.reshape(B, L, D)
```

```python
import functools
import math

import jax
import jax.numpy as jnp
from jax import lax
from jax.experimental import pallas as pl
from jax.experimental.pallas import tpu as pltpu

F32 = jnp.float32
BF16 = jnp.bfloat16

D_MODEL = 2048
S5_WIDTH = 1024
S5_GROUP = 16
S5_GROUPS = 64
S5_STATE = 64
DN_HEADS = 8
DN_DK = 128
DN_DV = 128
DN_QK_WIDTH = DN_HEADS * DN_DK
DN_V_WIDTH = DN_HEADS * DN_DV
DN_CONV = 4
DN_CHUNK = 64
DN_SOLVE_BLOCK = 16
FFN_DIM = 5632
FFN_CONV = 3
NORM_EPS = 1e-6

OFF_U = S5_WIDTH
OFF_QKV = OFF_U + 2 * DN_QK_WIDTH + DN_V_WIDTH
OFF_Z = OFF_QKV + DN_V_WIDTH
OFF_BETA = OFF_Z + DN_HEADS
OFF_ALPHA = OFF_BETA + DN_HEADS
OFF_GS = OFF_ALPHA + D_MODEL
N_IN = OFF_GS + D_MODEL
N_MAIN = N_IN - 2 * DN_HEADS

LANES = 128
SUBLANES = 8
VMEM_LIMIT = 56 * 1024 * 1024

S5_SEGS = SUBLANES
S5_GB = 8
S5_NBLK = S5_GROUPS // S5_GB
S5_BLK_CH = S5_GB * S5_GROUP
S5_BLK_ST = S5_GB * S5_STATE

HI = lax.Precision.HIGHEST


def _cparams(sem):
    return pltpu.CompilerParams(dimension_semantics=sem, vmem_limit_bytes=VMEM_LIMIT)


def _sigmoid(x):
    return 1.0 / (1.0 + jnp.exp(-x))


def _silu(x):
    hx = 0.5 * x
    return hx + hx * jnp.tanh(hx)


def _softplus(x):
    return jnp.maximum(x, 0.0) + jnp.log1p(jnp.exp(-jnp.abs(x)))


def _rms_rows(x, w):
    ms = jnp.mean(x * x, axis=-1, keepdims=True)
    return x * lax.rsqrt(ms + NORM_EPS) * w


def _rmsnorm_kernel(x_ref, w_ref, o_ref):
    o_ref[...] = _rms_rows(x_ref[...], w_ref[...]).astype(o_ref.dtype)


def rmsnorm_cast(x, w, tm=512):
    L, D = x.shape
    return pl.pallas_call(
        _rmsnorm_kernel,
        grid=(L // tm,),
        in_specs=[pl.BlockSpec((tm, D), lambda m: (m, 0)),
                  pl.BlockSpec((1, D), lambda m: (0, 0))],
        out_specs=pl.BlockSpec((tm, D), lambda m: (m, 0)),
        out_shape=jax.ShapeDtypeStruct((L, D), BF16),
        compiler_params=_cparams(("parallel",)),
        name="rmsnorm",
    )(x, w.reshape(1, D))


def _dot_nt(a, b):
    return lax.dot_general(a, b, (((1,), (1,)), ((), ())), preferred_element_type=F32)


def _in_proj_kernel(h_ref, wt_ref, o_ref, w16_ref):
    @pl.when(pl.program_id(1) == 0)
    def _():
        w16_ref[...] = wt_ref[...].astype(BF16)

    o_ref[...] = _dot_nt(h_ref[...], w16_ref[...]).astype(o_ref.dtype)


def in_proj(h, w_in_t, layer, tm=1024, tn=1024):
    L, D = h.shape
    n_mix = OFF_Z // tn

    def w_index(n, m):
        row = n * tn + jnp.where(n >= n_mix, OFF_ALPHA - OFF_Z, 0)
        return layer, pl.multiple_of(row, 2 * DN_HEADS), 0

    return pl.pallas_call(
        _in_proj_kernel,
        grid=(N_MAIN // tn, L // tm),
        in_specs=[pl.BlockSpec((tm, D), lambda n, m: (m, 0)),
                  pl.BlockSpec((None, pl.Element(tn), pl.Element(D)), w_index)],
        out_specs=pl.BlockSpec((tm, tn), lambda n, m: (m, n)),
        out_shape=jax.ShapeDtypeStruct((L, N_MAIN), BF16),
        scratch_shapes=[pltpu.VMEM((tn, D), BF16)],
        compiler_params=_cparams(("parallel", "arbitrary")),
        name="in_proj",
    )(h, w_in_t)


def _s5_disc_kernel(ldt_ref, are_ref, aim_ref, bre_ref, bim_ref,
                    abr_ref, abi_ref, bbr_ref, bbi_ref):
    lr = are_ref[...]
    li = aim_ref[...]
    dt = jnp.exp(ldt_ref[...])
    mag = jnp.exp(lr * dt)
    abar_re = mag * jnp.cos(li * dt)
    abar_im = mag * jnp.sin(li * dt)
    den = lr * lr + li * li
    nr = abar_re - 1.0
    ni = abar_im
    coef_re = (nr * lr + ni * li) / den
    coef_im = (ni * lr - nr * li) / den
    br = bre_ref[...]
    bi = bim_ref[...]
    abr_ref[...] = abar_re
    abi_ref[...] = abar_im
    bbr_ref[...] = coef_re * br - coef_im * bi
    bbi_ref[...] = coef_re * bi + coef_im * br


def s5_discretise(log_dt, a_re, a_im, b_re, b_im):
    P, HG = S5_STATE, S5_GROUP
    dg = log_dt.size
    bt_re = jnp.transpose(b_re.reshape(dg, P, HG), (0, 2, 1))
    bt_im = jnp.transpose(b_im.reshape(dg, P, HG), (0, 2, 1))
    return pl.pallas_call(
        _s5_disc_kernel,
        out_shape=[jax.ShapeDtypeStruct((dg, 1, P), F32),
                   jax.ShapeDtypeStruct((dg, 1, P), F32),
                   jax.ShapeDtypeStruct((dg, HG, P), F32),
                   jax.ShapeDtypeStruct((dg, HG, P), F32)],
        name="s5_disc",
    )(jnp.broadcast_to(log_dt.reshape(dg, 1, 1), (dg, 1, P)), a_re.reshape(dg, 1, P), a_im.reshape(dg, 1, P),
      bt_re, bt_im)


def _block_diag(t, rows_per_blk, cols_per_blk):
    tiled = jnp.tile(t, (1,) * (t.ndim - 1) + (S5_GB,))
    r = lax.broadcasted_iota(jnp.int32, tiled.shape, t.ndim - 2) // rows_per_blk
    c = lax.broadcasted_iota(jnp.int32, tiled.shape, t.ndim - 1) // cols_per_blk
    return jnp.where(r == c, tiled, jnp.zeros_like(tiled))


def s5_layout_params(abar_re, abar_im, bbar_re, bbar_im, c_re, c_im, depth):
    P, HG = S5_STATE, S5_GROUP
    ns = S5_GROUPS * P
    a_r = jnp.broadcast_to(abar_re.reshape(depth, 1, ns), (depth, S5_SEGS, ns))
    a_i = jnp.broadcast_to(abar_im.reshape(depth, 1, ns), (depth, S5_SEGS, ns))
    bb_re = bbar_re.reshape(depth, S5_NBLK, S5_BLK_CH, P)
    bb_im = bbar_im.reshape(depth, S5_NBLK, S5_BLK_CH, P)
    wb = jnp.concatenate([_block_diag(bb_re, HG, P), _block_diag(bb_im, HG, P)], axis=-1).astype(BF16)
    ct_re = jnp.transpose(c_re, (0, 1, 3, 2)).reshape(depth, S5_NBLK, S5_BLK_ST, HG)
    ct_im = jnp.transpose(c_im, (0, 1, 3, 2)).reshape(depth, S5_NBLK, S5_BLK_ST, HG)
    wc = jnp.concatenate([_block_diag(ct_re, P, HG), -_block_diag(ct_im, P, HG)], axis=-2).astype(BF16)
    return a_r, a_i, wb, wc


def _s5_kernel(u_ref, p_ref, pt_ref, wb_ref, ar_ref, ai_ref, wc_ref, d_ref, y_ref,
               bu_ref, x_ref, sr_ref, si_ref, *, ti, seg_len):
    R = S5_SEGS * ti
    NS = S5_BLK_ST
    ph = pl.program_id(0)
    it = pl.program_id(1)

    @pl.when(jnp.logical_and(ph == 0, it == 0))
    def _():
        sr_ref[...] = jnp.zeros_like(sr_ref)
        si_ref[...] = jnp.zeros_like(si_ref)

    @pl.when(jnp.logical_and(ph == 1, it == 0))
    def _():
        pr, pi = ar_ref[...], ai_ref[...]
        for _ in range(int(math.log2(seg_len))):
            pr, pi = pr * pr - pi * pi, 2.0 * pr * pi
        er, ei = sr_ref[...], si_ref[...]
        row = lax.broadcasted_iota(jnp.int32, er.shape, 0)
        zr = jnp.zeros_like(er)
        zi = jnp.zeros_like(ei)
        for _ in range(S5_SEGS - 1):
            nr = pr * zr - pi * zi + er
            ni = pr * zi + pi * zr + ei
            zr = jnp.where(row >= 1, pltpu.roll(nr, 1, 0), 0.0)
            zi = jnp.where(row >= 1, pltpu.roll(ni, 1, 0), 0.0)
        sr_ref[...] = zr
        si_ref[...] = zi

    u_nat = u_ref[...].reshape(R, S5_WIDTH)
    u2 = jnp.dot(p_ref[...], u_nat, preferred_element_type=F32).astype(BF16)
    for b in range(S5_NBLK):
        bu_ref[:, b * 2 * NS:(b + 1) * 2 * NS] = jnp.dot(
            u2[:, b * S5_BLK_CH:(b + 1) * S5_BLK_CH], wb_ref[b], preferred_element_type=F32)

    def scan(store):
        for b in range(S5_NBLK):
            ar = ar_ref[:, b * NS:(b + 1) * NS]
            ai = ai_ref[:, b * NS:(b + 1) * NS]
            c_re = b * 2 * NS
            c_im = c_re + NS

            def body(i, carry):
                xr, xi = carry
                r0 = pl.multiple_of(i * SUBLANES, SUBLANES)
                br = bu_ref[pl.ds(r0, SUBLANES), c_re:c_re + NS]
                bi = bu_ref[pl.ds(r0, SUBLANES), c_im:c_im + NS]
                nr = ar * xr - ai * xi + br
                ni = ar * xi + ai * xr + bi
                if store:
                    x_ref[pl.ds(r0, SUBLANES), c_re:c_re + NS] = nr
                    x_ref[pl.ds(r0, SUBLANES), c_im:c_im + NS] = ni
                return nr, ni

            xr, xi = lax.fori_loop(
                0, ti, body, (sr_ref[:, b * NS:(b + 1) * NS], si_ref[:, b * NS:(b + 1) * NS]), unroll=4)
            sr_ref[:, b * NS:(b + 1) * NS] = xr
            si_ref[:, b * NS:(b + 1) * NS] = xi

    @pl.when(ph == 0)
    def _():
        scan(False)

    @pl.when(ph == 1)
    def _():
        scan(True)
        ys = []
        for b in range(S5_NBLK):
            xb = x_ref[:, b * 2 * NS:(b + 1) * 2 * NS].astype(BF16)
            ys.append(jnp.dot(xb, wc_ref[b], preferred_element_type=F32))
        y2 = jnp.concatenate(ys, axis=1)
        y_hi = y2.astype(BF16)
        y_lo = (y2 - y_hi.astype(F32)).astype(BF16)
        pt = pt_ref[...]
        y_nat = (jnp.dot(pt, y_hi, preferred_element_type=F32)
                 + jnp.dot(pt, y_lo, preferred_element_type=F32))
        y_nat = y_nat + d_ref[...] * u_nat.astype(F32)
        y = 0.5 * y_nat * (1.0 + lax.erf(y_nat * (1.0 / math.sqrt(2.0))))
        y_ref[...] = y.reshape(S5_SEGS, ti, S5_WIDTH).astype(y_ref.dtype)


def s5_perm(ti):
    R = S5_SEGS * ti
    r = lax.broadcasted_iota(jnp.int32, (R, R), 0)
    c = lax.broadcasted_iota(jnp.int32, (R, R), 1)
    perm = (c == (r % S5_SEGS) * ti + r // S5_SEGS).astype(BF16)
    return perm, perm.T


def s5_scan(proj, perm, perm_t, a_r, a_i, wb, wc, d, layer, ti=32):
    L = proj.shape[0]
    seg_len = L // S5_SEGS
    assert seg_len * S5_SEGS == L and seg_len % ti == 0 and (seg_len & (seg_len - 1)) == 0
    R = S5_SEGS * ti
    nt = seg_len // ti
    proj3 = proj.reshape(S5_SEGS, seg_len, proj.shape[1])
    ns = S5_GROUPS * S5_STATE
    kern = functools.partial(_s5_kernel, ti=ti, seg_len=seg_len)
    y = pl.pallas_call(
        kern,
        grid=(2, nt),
        in_specs=[pl.BlockSpec((S5_SEGS, ti, S5_WIDTH), lambda p, t: (0, t, 0)),
                  pl.BlockSpec((R, R), lambda p, t: (0, 0)),
                  pl.BlockSpec((R, R), lambda p, t: (0, 0)),
                  pl.BlockSpec((None, S5_NBLK, S5_BLK_CH, 2 * S5_BLK_ST), lambda p, t: (layer, 0, 0, 0)),
                  pl.BlockSpec((None, S5_SEGS, ns), lambda p, t: (layer, 0, 0)),
                  pl.BlockSpec((None, S5_SEGS, ns), lambda p, t: (layer, 0, 0)),
                  pl.BlockSpec((None, S5_NBLK, 2 * S5_BLK_ST, S5_BLK_CH), lambda p, t: (layer, 0, 0, 0)),
                  pl.BlockSpec((None, 1, S5_WIDTH), lambda p, t: (layer, 0, 0))],
        out_specs=pl.BlockSpec((S5_SEGS, ti, S5_WIDTH), lambda p, t: (0, t * p, 0)),
        out_shape=jax.ShapeDtypeStruct((S5_SEGS, seg_len, S5_WIDTH), BF16),
        scratch_shapes=[pltpu.VMEM((R, 2 * ns), F32),
                        pltpu.VMEM((R, 2 * ns), F32),
                        pltpu.VMEM((S5_SEGS, ns), F32),
                        pltpu.VMEM((S5_SEGS, ns), F32)],
        compiler_params=_cparams(("arbitrary", "arbitrary")),
        name="s5_scan",
    )(proj3, perm, perm_t, wb, a_r, a_i, wc, d.reshape(-1, 1, S5_WIDTH))
    return y.reshape(L, S5_WIDTH)


def _dot_tn(a, b, precision=None):
    return lax.dot_general(a, b, (((0,), (0,)), ((), ())), preferred_element_type=F32,
                           precision=precision)


def _dn_kernel(q_ref, k_ref, v_ref, z_ref, h_ref, wba_ref, cw_ref, alog_ref, dtb_ref, nw_ref, o_ref,
               ext_ref, act_ref, gate_ref, s_ref, wba16_ref, *, tm):
    C = DN_CHUNK
    W = DN_QK_WIDTH
    it = pl.program_id(0)

    @pl.when(it == 0)
    def _():
        ext_ref[0:SUBLANES, :] = jnp.zeros((SUBLANES, 3 * W), F32)
        s_ref[...] = jnp.zeros_like(s_ref)
        wba16_ref[...] = jnp.zeros_like(wba16_ref)
        wba16_ref[0:2 * DN_HEADS, :] = wba_ref[...].astype(BF16)

    ext_ref[SUBLANES:SUBLANES + tm, 0:W] = q_ref[...].astype(F32)
    ext_ref[SUBLANES:SUBLANES + tm, W:2 * W] = k_ref[...].astype(F32)
    ext_ref[SUBLANES:SUBLANES + tm, 2 * W:3 * W] = v_ref[...].astype(F32)
    cw = cw_ref[...]
    for cb in range(3 * W // LANES):
        cols = slice(cb * LANES, (cb + 1) * LANES)
        x = ext_ref[:, cols]
        acc = cw[DN_CONV - 1:DN_CONV, cols] * x[SUBLANES:]
        for back in range(1, DN_CONV):
            acc = acc + cw[DN_CONV - 1 - back:DN_CONV - back, cols] * pltpu.roll(x, back, 0)[SUBLANES:]
        act_ref[:, cols] = _silu(acc)
    ext_ref[0:SUBLANES, :] = ext_ref[tm:tm + SUBLANES, :]

    ba = _dot_nt(h_ref[...], wba16_ref[...])
    lane = lax.broadcasted_iota(jnp.int32, ba.shape, 1)
    beta = _sigmoid(ba)
    g = -jnp.exp(alog_ref[...]) * _softplus(ba + dtb_ref[...])
    gate_ref[...] = jnp.where(lane < DN_HEADS, beta, g)

    ri = lax.broadcasted_iota(jnp.int32, (C, 2 * C), 0)
    cj = lax.broadcasted_iota(jnp.int32, (C, 2 * C), 1)
    right = cj >= C
    ci = jnp.where(right, cj - C, cj)
    tril = ri >= ci
    strict = ri > ci
    eye_f = (ri == ci).astype(F32)
    blk = DN_SOLVE_BLOCK
    same_blk = (ri // blk) == (ci // blk)
    sub_blk = ((ri // blk) == (ci // blk) + 1) & ((ri // (2 * blk)) == (ci // (2 * blk)))
    low_half = (ri >= C // 2) & (ci < C // 2)
    right_row = right[0:1, :]
    rs = lax.broadcasted_iota(jnp.int32, (C, C), 0)
    cs = lax.broadcasted_iota(jnp.int32, (C, C), 1)
    tril_f = (rs >= cs).astype(F32)
    triu2_f = (ri <= ci).astype(F32)
    nw = nw_ref[...]
    heads = range(DN_HEADS)
    pairs = range(DN_HEADS // 2)

    def bd(x):
        z = jnp.zeros_like(x)
        return jnp.concatenate([jnp.where(right, z, x), jnp.where(right, x, z)], axis=0)

    def bd_rows(a, b):
        za = jnp.zeros_like(a)
        return jnp.concatenate([jnp.concatenate([a, za], axis=1), jnp.concatenate([za, b], axis=1)], axis=0)

    def split(x):
        hi = x.astype(BF16)
        return hi, (x - hi.astype(F32)).astype(BF16)

    def dot3(a, b):
        (ah, al), (bh, bl) = a, b
        bdh = bd(bh)
        return (jnp.dot(ah, bdh, preferred_element_type=F32)
                + jnp.dot(al, bdh, preferred_element_type=F32)
                + jnp.dot(ah, bd(bl), preferred_element_type=F32))

    def bdot(a, b):
        return jnp.dot(a.astype(BF16), bd(b.astype(BF16)), preferred_element_type=F32)

    chunks = range(tm // C)
    hu = [(c, h) for c in chunks for h in heads]
    pu = [(c, p) for c in chunks for p in pairs]
    gt = {c: gate_ref[c * C:(c + 1) * C, :] for c in chunks}
    gc = {c: jnp.dot(tril_f, gt[c], preferred_element_type=F32, precision=HI) for c in chunks}
    gc_t = {c: _dot_tn(gt[c], triu2_f, precision=HI) for c in chunks}
    q, k, kb, vb, gcol, glast, eg = {}, {}, {}, {}, {}, {}, {}
    for c, h in hu:
        rows = slice(c * C, (c + 1) * C)
        lo = h * DN_DK
        qh = act_ref[rows, lo:lo + DN_DK]
        kh = act_ref[rows, W + lo:W + lo + DN_DK]
        vh = act_ref[rows, 2 * W + lo:2 * W + lo + DN_DV]
        q[c, h] = qh * lax.rsqrt(jnp.sum(qh * qh, axis=-1, keepdims=True) + NORM_EPS) * (DN_DK ** -0.5)
        kh = kh * lax.rsqrt(jnp.sum(kh * kh, axis=-1, keepdims=True) + NORM_EPS)
        k[c, h] = kh
        bcol = gt[c][:, h:h + 1]
        gc_h = gc[c][:, DN_HEADS + h:DN_HEADS + h + 1]
        gcol[c, h] = gc_h
        glast[c, h] = gc_h[C - 1:C, :]
        eg[c, h] = jnp.exp(gc_h)
        kb[c, h] = kh * bcol
        vb[c, h] = vh * bcol
    k16 = {u: k[u].astype(BF16) for u in hu}
    kbd = {(c, p): bd_rows(k16[c, 2 * p], k16[c, 2 * p + 1]) for c, p in pu}
    kb2 = {(c, p): jnp.concatenate([kb[c, 2 * p], kb[c, 2 * p + 1]], axis=1).astype(BF16) for c, p in pu}
    q2 = {(c, p): jnp.concatenate([q[c, 2 * p], q[c, 2 * p + 1]], axis=1).astype(BF16) for c, p in pu}
    kk = {u: _dot_nt(kb2[u], kbd[u]) for u in pu}
    qk = {u: _dot_nt(q2[u], kbd[u]) for u in pu}
    dec = {}
    for c, p in pu:
        a, b = 2 * p, 2 * p + 1
        gcol2 = jnp.where(right, gcol[c, b], gcol[c, a])
        grow2 = jnp.where(right_row, gc_t[c][DN_HEADS + b:DN_HEADS + b + 1, :],
                          gc_t[c][DN_HEADS + a:DN_HEADS + a + 1, :])
        dec[c, p] = jnp.where(tril, jnp.exp(jnp.where(tril, gcol2 - grow2, 0.0)), 0.0)
    lmat = {u: jnp.where(strict, kk[u] * dec[u], 0.0) for u in pu}
    attn = {u: (qk[u] * dec[u]).astype(BF16) for u in pu}

    pw = {u: jnp.where(same_blk, -lmat[u], 0.0) for u in pu}
    tinv = {u: eye_f + pw[u] for u in pu}
    pws = {u: split(pw[u]) for u in pu}
    for _ in range(int(math.log2(blk)) - 1):
        pw = {u: dot3(pws[u], pws[u]) for u in pu}
        pws = {u: split(pw[u]) for u in pu}
        tinv = {u: tinv[u] + dot3(split(tinv[u]), pws[u]) for u in pu}
    x1 = {u: bdot(jnp.where(sub_blk, lmat[u], 0.0), tinv[u]) for u in pu}
    tinv = {u: tinv[u] - bdot(tinv[u], x1[u]) for u in pu}
    x2 = {u: bdot(jnp.where(low_half, lmat[u], 0.0), tinv[u]) for u in pu}
    tinv = {u: tinv[u] - bdot(tinv[u], x2[u]) for u in pu}

    rhs = {u: jnp.concatenate([vb[u], kb[u] * eg[u]], axis=1).astype(BF16) for u in hu}
    sol2 = {(c, p): jnp.dot(tinv[c, p].astype(BF16), bd_rows(rhs[c, 2 * p], rhs[c, 2 * p + 1]),
                            preferred_element_type=F32) for c, p in pu}
    qe = {u: q[u] * eg[u] for u in hu}
    kd = {u: (k[u] * jnp.exp(glast[u] - gcol[u])).astype(BF16) for u in hu}

    s = [s_ref[h] for h in heads]
    for c in chunks:
        rows = slice(c * C, (c + 1) * C)
        u_c = [sol2[c, h // 2][:, (h % 2) * 2 * DN_DV:(h % 2) * 2 * DN_DV + DN_DV] for h in heads]
        w_c = [sol2[c, h // 2][:, (h % 2) * 2 * DN_DV + DN_DV:(h % 2 + 1) * 2 * DN_DV] for h in heads]
        s16 = [s[h].astype(BF16) for h in heads]
        wq = [jnp.concatenate([w_c[h], qe[c, h]], axis=0).astype(BF16) for h in heads]
        wqs = [jnp.dot(wq[h], s16[h], preferred_element_type=F32) for h in heads]
        vn16 = [(u_c[h] - wqs[h][:C, :]).astype(BF16) for h in heads]
        av2 = [jnp.dot(attn[c, p], bd_rows(vn16[2 * p], vn16[2 * p + 1]), preferred_element_type=F32)
               for p in pairs]
        s = [s[h] * jnp.exp(glast[c, h]) + _dot_tn(kd[c, h], vn16[h]) for h in heads]
        for h in heads:
            lo = h * DN_DV
            zh = z_ref[rows, lo:lo + DN_DV].astype(F32)
            oh = wqs[h][C:, :] + av2[h // 2][:, (h % 2) * DN_DV:(h % 2 + 1) * DN_DV]
            on = oh * lax.rsqrt(jnp.mean(oh * oh, axis=-1, keepdims=True) + NORM_EPS) * nw
            o_ref[rows, lo:lo + DN_DV] = (on * _silu(zh)).astype(o_ref.dtype)
    for h in heads:
        s_ref[h] = s[h]


def deltanet(proj, h, w_in_t, layer, conv_w, a_log, dt_bias, norm_w, tm=256):
    L = proj.shape[0]
    D = h.shape[1]
    W = DN_QK_WIDTH
    pad = jnp.zeros((1, LANES), F32)
    alog_row = pad.at[0, DN_HEADS:2 * DN_HEADS].set(a_log)
    dtb_row = pad.at[0, DN_HEADS:2 * DN_HEADS].set(dt_bias)
    kern = functools.partial(_dn_kernel, tm=tm)
    return pl.pallas_call(
        kern,
        grid=(L // tm,),
        in_specs=[pl.BlockSpec((tm, W), lambda t: (t, 1)),
                  pl.BlockSpec((tm, W), lambda t: (t, 2)),
                  pl.BlockSpec((tm, W), lambda t: (t, 3)),
                  pl.BlockSpec((tm, W), lambda t: (t, 4)),
                  pl.BlockSpec((tm, D), lambda t: (t, 0)),
                  pl.BlockSpec((None, pl.Element(2 * DN_HEADS), pl.Element(D)), lambda t: (layer, OFF_Z, 0)),
                  pl.BlockSpec((DN_CONV, 3 * W), lambda t: (0, 0)),
                  pl.BlockSpec((1, LANES), lambda t: (0, 0)),
                  pl.BlockSpec((1, LANES), lambda t: (0, 0)),
                  pl.BlockSpec((1, DN_DV), lambda t: (0, 0))],
        out_specs=pl.BlockSpec((tm, DN_V_WIDTH), lambda t: (t, 0)),
        out_shape=jax.ShapeDtypeStruct((L, DN_V_WIDTH), BF16),
        scratch_shapes=[pltpu.VMEM((tm + SUBLANES, 3 * W), F32),
                        pltpu.VMEM((tm, 3 * W), F32),
                        pltpu.VMEM((tm, LANES), F32),
                        pltpu.VMEM((DN_HEADS, DN_DK, DN_DV), F32),
                        pltpu.VMEM((LANES, D), BF16)],
        compiler_params=_cparams(("arbitrary",)),
        name="deltanet",
    )(proj, proj, proj, proj, h, w_in_t, conv_w, alog_row, dtb_row, norm_w.reshape(1, DN_DV))


def _merge_kernel(ys_ref, yd_ref, wga_ref, wgb_ref, wdn_ref, gs_ref, gd_ref, o_ref,
                  wga16_ref, wgb16_ref, wdn16_ref):
    @pl.when(pl.program_id(1) == 0)
    def _():
        wga16_ref[...] = wga_ref[...].astype(BF16)
        wgb16_ref[...] = wgb_ref[...].astype(BF16)
        wdn16_ref[...] = wdn_ref[...].astype(BF16)

    ys = ys_ref[...]
    ga = jnp.dot(ys, wga16_ref[...], preferred_element_type=F32)
    gb = jnp.dot(ys, wgb16_ref[...], preferred_element_type=F32)
    dn = jnp.dot(yd_ref[...], wdn16_ref[...], preferred_element_type=F32)
    br_s5 = ga * _sigmoid(gb)
    gs = gs_ref[...].astype(F32)
    gd = gd_ref[...].astype(F32)
    o_ref[...] = (_sigmoid(gs) * br_s5 + _sigmoid(gd) * dn).astype(o_ref.dtype)


def merge(y_s5, y_dn, glu_w, dn_w, proj, layer, tm=1024, tn=512):
    L = y_s5.shape[0]
    nb = D_MODEL // tn
    gs_blk = OFF_Z // tn
    gd_blk = gs_blk + nb
    return pl.pallas_call(
        _merge_kernel,
        grid=(nb, L // tm),
        in_specs=[pl.BlockSpec((tm, S5_WIDTH), lambda n, m: (m, 0)),
                  pl.BlockSpec((tm, DN_V_WIDTH), lambda n, m: (m, 0)),
                  pl.BlockSpec((None, S5_WIDTH, tn), lambda n, m: (layer, 0, n)),
                  pl.BlockSpec((None, S5_WIDTH, tn), lambda n, m: (layer, 0, n + nb)),
                  pl.BlockSpec((None, DN_V_WIDTH, tn), lambda n, m: (layer, 0, n)),
                  pl.BlockSpec((tm, tn), lambda n, m: (m, n + gs_blk)),
                  pl.BlockSpec((tm, tn), lambda n, m: (m, n + gd_blk))],
        out_specs=pl.BlockSpec((tm, tn), lambda n, m: (m, n)),
        out_shape=jax.ShapeDtypeStruct((L, D_MODEL), BF16),
        scratch_shapes=[pltpu.VMEM((S5_WIDTH, tn), BF16),
                        pltpu.VMEM((S5_WIDTH, tn), BF16),
                        pltpu.VMEM((DN_V_WIDTH, tn), BF16)],
        compiler_params=_cparams(("parallel", "arbitrary")),
        name="merge",
    )(y_s5, y_dn, glu_w, glu_w, dn_w, proj, proj)


def _out_proj_kernel(m_ref, w_ref, x_ref, nw_ref, xo_ref, h_ref, w16_ref):
    @pl.when(pl.program_id(0) == 0)
    def _():
        w16_ref[...] = w_ref[...].astype(BF16)

    xo = x_ref[...] + jnp.dot(m_ref[...], w16_ref[...], preferred_element_type=F32)
    xo_ref[...] = xo
    h_ref[...] = _rms_rows(xo, nw_ref[...]).astype(h_ref.dtype)


def out_proj(merged, w_out, x, norm_w, layer, tm=512):
    L, D = x.shape
    return pl.pallas_call(
        _out_proj_kernel,
        grid=(L // tm,),
        in_specs=[pl.BlockSpec((tm, D), lambda m: (m, 0)),
                  pl.BlockSpec((None, D, D), lambda m: (layer, 0, 0), pipeline_mode=pl.Buffered(1)),
                  pl.BlockSpec((tm, D), lambda m: (m, 0)),
                  pl.BlockSpec((1, D), lambda m: (0, 0))],
        out_specs=[pl.BlockSpec((tm, D), lambda m: (m, 0)),
                   pl.BlockSpec((tm, D), lambda m: (m, 0))],
        out_shape=[jax.ShapeDtypeStruct((L, D), F32),
                   jax.ShapeDtypeStruct((L, D), BF16)],
        scratch_shapes=[pltpu.VMEM((D, D), BF16)],
        compiler_params=_cparams(("arbitrary",)),
        name="out_proj",
    )(merged, w_out, x, norm_w.reshape(1, D))


def _ffn_up_kernel(h_ref, wa_ref, wv_ref, cwa_ref, cwv_ref, o_ref, ext_ref, w16_ref, *, tm, tn):
    @pl.when(pl.program_id(1) == 0)
    def _():
        ext_ref[...] = jnp.zeros_like(ext_ref)
        w16_ref[:, :tn] = wa_ref[...].astype(BF16)
        w16_ref[:, tn:] = wv_ref[...].astype(BF16)

    up = jnp.dot(h_ref[...], w16_ref[...], preferred_element_type=F32)
    cw = jnp.concatenate([cwa_ref[...], cwv_ref[...]], axis=1)
    prev = ext_ref[...]
    r8 = lax.broadcasted_iota(jnp.int32, prev.shape, 0)
    acc = cw[FFN_CONV - 1:FFN_CONV, :] * up
    for back in range(1, FFN_CONV):
        sh = pltpu.roll(up, back, 0)
        head = jnp.where(r8 < back, pltpu.roll(prev, back, 0), sh[:SUBLANES])
        sh = jnp.concatenate([head, sh[SUBLANES:]], axis=0)
        acc = acc + cw[FFN_CONV - 1 - back:FFN_CONV - back, :] * sh
    o_ref[...] = (_silu(acc[:, :tn]) * acc[:, tn:]).astype(o_ref.dtype)
    ext_ref[...] = up[tm - SUBLANES:, :]


def ffn_up_call(h, w_up, conv_w, layer, tm=1024, tn=512):
    L, D = h.shape
    nb = FFN_DIM // tn
    kern = functools.partial(_ffn_up_kernel, tm=tm, tn=tn)
    return pl.pallas_call(
        kern,
        grid=(nb, L // tm),
        in_specs=[pl.BlockSpec((tm, D), lambda n, m: (m, 0)),
                  pl.BlockSpec((None, D, tn), lambda n, m: (layer, 0, n)),
                  pl.BlockSpec((None, D, tn), lambda n, m: (layer, 0, n + nb)),
                  pl.BlockSpec((FFN_CONV, tn), lambda n, m: (0, n)),
                  pl.BlockSpec((FFN_CONV, tn), lambda n, m: (0, n + nb))],
        out_specs=pl.BlockSpec((tm, tn), lambda n, m: (m, n)),
        out_shape=jax.ShapeDtypeStruct((L, FFN_DIM), BF16),
        scratch_shapes=[pltpu.VMEM((SUBLANES, 2 * tn), F32),
                        pltpu.VMEM((D, 2 * tn), BF16)],
        compiler_params=_cparams(("parallel", "arbitrary")),
        name="ffn_up",
    )(h, w_up, w_up, conv_w, conv_w)


def _ffn_down_kernel(a_ref, w_ref, x_ref, nw_ref, *refs, last):
    acc_ref = refs[0]
    k = pl.program_id(1)

    @pl.when(k == 0)
    def _():
        acc_ref[...] = x_ref[...]

    acc_ref[...] += jnp.dot(a_ref[...], w_ref[...], preferred_element_type=F32)

    @pl.when(k == pl.num_programs(1) - 1)
    def _():
        xo = acc_ref[...]
        if last:
            acc_ref[...] = _rms_rows(xo, nw_ref[...])
        else:
            refs[1][...] = _rms_rows(xo, nw_ref[...]).astype(refs[1].dtype)


def ffn_down_call(a, w_down, x, norm_w, layer, last, tm=1024, tk=512):
    L, D = x.shape
    K = a.shape[1]
    row = pl.BlockSpec((tm, D), lambda m, k: (m, 0))
    if last:
        out_specs = [row]
        out_shape = [jax.ShapeDtypeStruct((L, D), F32)]
    else:
        out_specs = [row, row]
        out_shape = [jax.ShapeDtypeStruct((L, D), F32), jax.ShapeDtypeStruct((L, D), BF16)]
    kern = functools.partial(_ffn_down_kernel, last=last)
    return pl.pallas_call(
        kern,
        grid=(L // tm, K // tk),
        in_specs=[pl.BlockSpec((tm, tk), lambda m, k: (m, k)),
                  pl.BlockSpec((None, tk, D), lambda m, k: (layer, k, 0)),
                  row,
                  pl.BlockSpec((1, D), lambda m, k: (0, 0))],
        out_specs=out_specs,
        out_shape=out_shape,
        compiler_params=_cparams(("parallel", "arbitrary")),
        name="ffn_down",
    )(a, w_down, x, norm_w.reshape(1, D))


def kernel(x, mix_norm_w, w_in, s5_log_dt, s5_a_re, s5_a_im, s5_b_re, s5_b_im, s5_c_re, s5_c_im, s5_d, s5_glu_w, dn_conv_w, dn_a_log, dn_dt_bias, dn_norm_w, dn_proj_w, w_out, ffn_norm_w, ffn_up, ffn_conv_w, ffn_down, final_norm_w):
    B, L, D = x.shape
    depth = w_in.shape[0]
    assert B == 1 and D == D_MODEL
    xs = x.reshape(L, D)
    h = rmsnorm_cast(xs, mix_norm_w[0])
    abr, abi, bbr, bbi = s5_discretise(s5_log_dt, s5_a_re, s5_a_im, s5_b_re, s5_b_im)
    s5_ar, s5_ai, s5_wb, s5_wc = s5_layout_params(abr, abi, bbr, bbi, s5_c_re, s5_c_im, depth)
    s5_ti = 32
    perm, perm_t = s5_perm(s5_ti)
    w_in_t = jnp.swapaxes(w_in, 1, 2)
    w_down = ffn_down.astype(BF16)
    for l in range(depth):
        proj = in_proj(h, w_in_t, l)

        y_s5 = s5_scan(proj, perm, perm_t, s5_ar, s5_ai, s5_wb, s5_wc, s5_d, l, ti=s5_ti)
        y_dn = deltanet(proj, h, w_in_t, l, dn_conv_w[l], dn_a_log[l], dn_dt_bias[l], dn_norm_w[l])

        merged = merge(y_s5, y_dn, s5_glu_w, dn_proj_w, proj, l)
        xs, h = out_proj(merged, w_out, xs, ffn_norm_w[l], l)
        act = ffn_up_call(h, ffn_up, ffn_conv_w[l], l)
        if l + 1 < depth:
            xs, h = ffn_down_call(act, w_down, xs, mix_norm_w[l + 1], l, last=False)
        else:
            (out,) = ffn_down_call(act, w_down, xs, final_norm_w, l, last=True)
    return out.reshape(B, L, D)
```

```python
import functools
import math

import jax
import jax.numpy as jnp
from jax import lax
from jax.experimental import pallas as pl
from jax.experimental.pallas import tpu as pltpu

F32 = jnp.float32
BF16 = jnp.bfloat16

D_MODEL = 2048
S5_WIDTH = 1024
S5_GROUP = 16
S5_GROUPS = 64
S5_STATE = 64
DN_HEADS = 8
DN_DK = 128
DN_DV = 128
DN_QK_WIDTH = DN_HEADS * DN_DK
DN_V_WIDTH = DN_HEADS * DN_DV
DN_CONV = 4
DN_CHUNK = 64
DN_SOLVE_BLOCK = 16
FFN_DIM = 5632
FFN_CONV = 3
NORM_EPS = 1e-6

OFF_U = S5_WIDTH
OFF_QKV = OFF_U + 2 * DN_QK_WIDTH + DN_V_WIDTH
OFF_Z = OFF_QKV + DN_V_WIDTH
OFF_BETA = OFF_Z + DN_HEADS
OFF_ALPHA = OFF_BETA + DN_HEADS
OFF_GS = OFF_ALPHA + D_MODEL
N_IN = OFF_GS + D_MODEL
N_MAIN = N_IN - 2 * DN_HEADS

LANES = 128
SUBLANES = 8
VMEM_LIMIT = 56 * 1024 * 1024

S5_SEGS = SUBLANES
S5_GB = 8
S5_NBLK = S5_GROUPS // S5_GB
S5_BLK_CH = S5_GB * S5_GROUP
S5_BLK_ST = S5_GB * S5_STATE

HI = lax.Precision.HIGHEST


def _cparams(sem):
    return pltpu.CompilerParams(dimension_semantics=sem, vmem_limit_bytes=VMEM_LIMIT)


def _sigmoid(x):
    return 1.0 / (1.0 + jnp.exp(-x))


def _silu(x):
    hx = 0.5 * x
    return hx + hx * jnp.tanh(hx)


def _softplus(x):
    return jnp.maximum(x, 0.0) + jnp.log1p(jnp.exp(-jnp.abs(x)))


def _rms_rows(x, w):
    ms = jnp.mean(x * x, axis=-1, keepdims=True)
    return x * lax.rsqrt(ms + NORM_EPS) * w


def _rmsnorm_kernel(x_ref, w_ref, o_ref):
    o_ref[...] = _rms_rows(x_ref[...], w_ref[...]).astype(o_ref.dtype)


def rmsnorm_cast(x, w, tm=512):
    L, D = x.shape
    return pl.pallas_call(
        _rmsnorm_kernel,
        grid=(L // tm,),
        in_specs=[pl.BlockSpec((tm, D), lambda m: (m, 0)),
                  pl.BlockSpec((1, D), lambda m: (0, 0))],
        out_specs=pl.BlockSpec((tm, D), lambda m: (m, 0)),
        out_shape=jax.ShapeDtypeStruct((L, D), BF16),
        compiler_params=_cparams(("parallel",)),
        name="rmsnorm",
    )(x, w.reshape(1, D))


def _dot_nt(a, b):
    return lax.dot_general(a, b, (((1,), (1,)), ((), ())), preferred_element_type=F32)


def _in_proj_kernel(h_ref, wt_ref, o_ref, w16_ref):
    @pl.when(pl.program_id(1) == 0)
    def _():
        w16_ref[...] = wt_ref[...].astype(BF16)

    o_ref[...] = _dot_nt(h_ref[...], w16_ref[...])


def in_proj(h, w_in_t, layer, tm=1024, tn=1024):
    L, D = h.shape
    n_mix = OFF_Z // tn

    def w_index(n, m):
        row = n * tn + jnp.where(n >= n_mix, OFF_ALPHA - OFF_Z, 0)
        return layer, pl.multiple_of(row, 2 * DN_HEADS), 0

    return pl.pallas_call(
        _in_proj_kernel,
        grid=(N_MAIN // tn, L // tm),
        in_specs=[pl.BlockSpec((tm, D), lambda n, m: (m, 0)),
                  pl.BlockSpec((None, pl.Element(tn), pl.Element(D)), w_index)],
        out_specs=pl.BlockSpec((tm, tn), lambda n, m: (m, n)),
        out_shape=jax.ShapeDtypeStruct((L, N_MAIN), F32),
        scratch_shapes=[pltpu.VMEM((tn, D), BF16)],
        compiler_params=_cparams(("parallel", "arbitrary")),
        name="in_proj",
    )(h, w_in_t)


def _s5_disc_kernel(ldt_ref, are_ref, aim_ref, bre_ref, bim_ref,
                    abr_ref, abi_ref, bbr_ref, bbi_ref):
    lr = are_ref[...]
    li = aim_ref[...]
    dt = jnp.exp(ldt_ref[...])
    mag = jnp.exp(lr * dt)
    abar_re = mag * jnp.cos(li * dt)
    abar_im = mag * jnp.sin(li * dt)
    den = lr * lr + li * li
    nr = abar_re - 1.0
    ni = abar_im
    coef_re = (nr * lr + ni * li) / den
    coef_im = (ni * lr - nr * li) / den
    br = bre_ref[...]
    bi = bim_ref[...]
    abr_ref[...] = abar_re
    abi_ref[...] = abar_im
    bbr_ref[...] = coef_re * br - coef_im * bi
    bbi_ref[...] = coef_re * bi + coef_im * br


def s5_discretise(log_dt, a_re, a_im, b_re, b_im):
    P, HG = S5_STATE, S5_GROUP
    dg = log_dt.size
    bt_re = jnp.transpose(b_re.reshape(dg, P, HG), (0, 2, 1))
    bt_im = jnp.transpose(b_im.reshape(dg, P, HG), (0, 2, 1))
    return pl.pallas_call(
        _s5_disc_kernel,
        out_shape=[jax.ShapeDtypeStruct((dg, 1, P), F32),
                   jax.ShapeDtypeStruct((dg, 1, P), F32),
                   jax.ShapeDtypeStruct((dg, HG, P), F32),
                   jax.ShapeDtypeStruct((dg, HG, P), F32)],
        name="s5_disc",
    )(jnp.broadcast_to(log_dt.reshape(dg, 1, 1), (dg, 1, P)), a_re.reshape(dg, 1, P), a_im.reshape(dg, 1, P),
      bt_re, bt_im)


def _block_diag(t, rows_per_blk, cols_per_blk):
    tiled = jnp.tile(t, (1,) * (t.ndim - 1) + (S5_GB,))
    r = lax.broadcasted_iota(jnp.int32, tiled.shape, t.ndim - 2) // rows_per_blk
    c = lax.broadcasted_iota(jnp.int32, tiled.shape, t.ndim - 1) // cols_per_blk
    return jnp.where(r == c, tiled, jnp.zeros_like(tiled))


def s5_layout_params(abar_re, abar_im, bbar_re, bbar_im, c_re, c_im, depth):
    P, HG = S5_STATE, S5_GROUP
    ns = S5_GROUPS * P
    a_r = jnp.broadcast_to(abar_re.reshape(depth, 1, ns), (depth, S5_SEGS, ns))
    a_i = jnp.broadcast_to(abar_im.reshape(depth, 1, ns), (depth, S5_SEGS, ns))
    bb_re = bbar_re.reshape(depth, S5_NBLK, S5_BLK_CH, P)
    bb_im = bbar_im.reshape(depth, S5_NBLK, S5_BLK_CH, P)
    wb = jnp.concatenate([_block_diag(bb_re, HG, P), _block_diag(bb_im, HG, P)], axis=-1).astype(BF16)
    ct_re = jnp.transpose(c_re, (0, 1, 3, 2)).reshape(depth, S5_NBLK, S5_BLK_ST, HG)
    ct_im = jnp.transpose(c_im, (0, 1, 3, 2)).reshape(depth, S5_NBLK, S5_BLK_ST, HG)
    wc = jnp.concatenate([_block_diag(ct_re, P, HG), -_block_diag(ct_im, P, HG)], axis=-2).astype(BF16)
    return a_r, a_i, wb, wc


def _s5_kernel(u_ref, p_ref, pt_ref, wb_ref, ar_ref, ai_ref, wc_ref, d_ref, y_ref,
               bu_ref, x_ref, sr_ref, si_ref, *, ti, seg_len):
    R = S5_SEGS * ti
    NS = S5_BLK_ST
    ph = pl.program_id(0)
    it = pl.program_id(1)

    @pl.when(jnp.logical_and(ph == 0, it == 0))
    def _():
        sr_ref[...] = jnp.zeros_like(sr_ref)
        si_ref[...] = jnp.zeros_like(si_ref)

    @pl.when(jnp.logical_and(ph == 1, it == 0))
    def _():
        pr, pi = ar_ref[...], ai_ref[...]
        for _ in range(int(math.log2(seg_len))):
            pr, pi = pr * pr - pi * pi, 2.0 * pr * pi
        er, ei = sr_ref[...], si_ref[...]
        row = lax.broadcasted_iota(jnp.int32, er.shape, 0)
        zr = jnp.zeros_like(er)
        zi = jnp.zeros_like(ei)
        for _ in range(S5_SEGS - 1):
            nr = pr * zr - pi * zi + er
            ni = pr * zi + pi * zr + ei
            zr = jnp.where(row >= 1, pltpu.roll(nr, 1, 0), 0.0)
            zi = jnp.where(row >= 1, pltpu.roll(ni, 1, 0), 0.0)
        sr_ref[...] = zr
        si_ref[...] = zi

    u_nat = u_ref[...].reshape(R, S5_WIDTH)
    u2 = jnp.dot(p_ref[...], u_nat.astype(BF16), preferred_element_type=F32).astype(BF16)
    for b in range(S5_NBLK):
        bu_ref[:, b * 2 * NS:(b + 1) * 2 * NS] = jnp.dot(
            u2[:, b * S5_BLK_CH:(b + 1) * S5_BLK_CH], wb_ref[b], preferred_element_type=F32)

    def scan(store):
        for b in range(S5_NBLK):
            ar = ar_ref[:, b * NS:(b + 1) * NS]
            ai = ai_ref[:, b * NS:(b + 1) * NS]
            c_re = b * 2 * NS
            c_im = c_re + NS

            def body(i, carry):
                xr, xi = carry
                r0 = pl.multiple_of(i * SUBLANES, SUBLANES)
                br = bu_ref[pl.ds(r0, SUBLANES), c_re:c_re + NS]
                bi = bu_ref[pl.ds(r0, SUBLANES), c_im:c_im + NS]
                nr = ar * xr - ai * xi + br
                ni = ar * xi + ai * xr + bi
                if store:
                    x_ref[pl.ds(r0, SUBLANES), c_re:c_re + NS] = nr
                    x_ref[pl.ds(r0, SUBLANES), c_im:c_im + NS] = ni
                return nr, ni

            xr, xi = lax.fori_loop(
                0, ti, body, (sr_ref[:, b * NS:(b + 1) * NS], si_ref[:, b * NS:(b + 1) * NS]), unroll=4)
            sr_ref[:, b * NS:(b + 1) * NS] = xr
            si_ref[:, b * NS:(b + 1) * NS] = xi

    @pl.when(ph == 0)
    def _():
        scan(False)

    @pl.when(ph == 1)
    def _():
        scan(True)
        ys = []
        for b in range(S5_NBLK):
            xb = x_ref[:, b * 2 * NS:(b + 1) * 2 * NS].astype(BF16)
            ys.append(jnp.dot(xb, wc_ref[b], preferred_element_type=F32))
        y2 = jnp.concatenate(ys, axis=1)
        y_hi = y2.astype(BF16)
        y_lo = (y2 - y_hi.astype(F32)).astype(BF16)
        pt = pt_ref[...]
        y_nat = (jnp.dot(pt, y_hi, preferred_element_type=F32)
                 + jnp.dot(pt, y_lo, preferred_element_type=F32))
        y_nat = y_nat + d_ref[...] * u_nat
        y = 0.5 * y_nat * (1.0 + lax.erf(y_nat * (1.0 / math.sqrt(2.0))))
        y_ref[...] = y.reshape(S5_SEGS, ti, S5_WIDTH).astype(y_ref.dtype)


def s5_perm(ti):
    R = S5_SEGS * ti
    r = lax.broadcasted_iota(jnp.int32, (R, R), 0)
    c = lax.broadcasted_iota(jnp.int32, (R, R), 1)
    perm = (c == (r % S5_SEGS) * ti + r // S5_SEGS).astype(BF16)
    return perm, perm.T


def s5_scan(proj, perm, perm_t, a_r, a_i, wb, wc, d, layer, ti=32):
    L = proj.shape[0]
    seg_len = L // S5_SEGS
    assert seg_len * S5_SEGS == L and seg_len % ti == 0 and (seg_len & (seg_len - 1)) == 0
    R = S5_SEGS * ti
    nt = seg_len // ti
    proj3 = proj.reshape(S5_SEGS, seg_len, proj.shape[1])
    ns = S5_GROUPS * S5_STATE
    kern = functools.partial(_s5_kernel, ti=ti, seg_len=seg_len)
    y = pl.pallas_call(
        kern,
        grid=(2, nt),
        in_specs=[pl.BlockSpec((S5_SEGS, ti, S5_WIDTH), lambda p, t: (0, t, 0)),
                  pl.BlockSpec((R, R), lambda p, t: (0, 0)),
                  pl.BlockSpec((R, R), lambda p, t: (0, 0)),
                  pl.BlockSpec((None, S5_NBLK, S5_BLK_CH, 2 * S5_BLK_ST), lambda p, t: (layer, 0, 0, 0)),
                  pl.BlockSpec((None, S5_SEGS, ns), lambda p, t: (layer, 0, 0)),
                  pl.BlockSpec((None, S5_SEGS, ns), lambda p, t: (layer, 0, 0)),
                  pl.BlockSpec((None, S5_NBLK, 2 * S5_BLK_ST, S5_BLK_CH), lambda p, t: (layer, 0, 0, 0)),
                  pl.BlockSpec((None, 1, S5_WIDTH), lambda p, t: (layer, 0, 0))],
        out_specs=pl.BlockSpec((S5_SEGS, ti, S5_WIDTH), lambda p, t: (0, t * p, 0)),
        out_shape=jax.ShapeDtypeStruct((S5_SEGS, seg_len, S5_WIDTH), BF16),
        scratch_shapes=[pltpu.VMEM((R, 2 * ns), F32),
                        pltpu.VMEM((R, 2 * ns), F32),
                        pltpu.VMEM((S5_SEGS, ns), F32),
                        pltpu.VMEM((S5_SEGS, ns), F32)],
        compiler_params=_cparams(("arbitrary", "arbitrary")),
        name="s5_scan",
    )(proj3, perm, perm_t, wb, a_r, a_i, wc, d.reshape(-1, 1, S5_WIDTH))
    return y.reshape(L, S5_WIDTH)


def _dot_tn(a, b, precision=None):
    return lax.dot_general(a, b, (((0,), (0,)), ((), ())), preferred_element_type=F32,
                           precision=precision)


def _dn_kernel(q_ref, k_ref, v_ref, z_ref, h_ref, wba_ref, cw_ref, alog_ref, dtb_ref, nw_ref, o_ref,
               ext_ref, act_ref, gate_ref, s_ref, wba16_ref, *, tm):
    C = DN_CHUNK
    W = DN_QK_WIDTH
    it = pl.program_id(0)

    @pl.when(it == 0)
    def _():
        ext_ref[0:SUBLANES, :] = jnp.zeros((SUBLANES, 3 * W), F32)
        s_ref[...] = jnp.zeros_like(s_ref)
        wba16_ref[...] = jnp.zeros_like(wba16_ref)
        wba16_ref[0:2 * DN_HEADS, :] = wba_ref[...].astype(BF16)

    ext_ref[SUBLANES:SUBLANES + tm, 0:W] = q_ref[...]
    ext_ref[SUBLANES:SUBLANES + tm, W:2 * W] = k_ref[...]
    ext_ref[SUBLANES:SUBLANES + tm, 2 * W:3 * W] = v_ref[...]
    cw = cw_ref[...]
    for cb in range(3 * W // LANES):
        cols = slice(cb * LANES, (cb + 1) * LANES)
        x = ext_ref[:, cols]
        acc = cw[DN_CONV - 1:DN_CONV, cols] * x[SUBLANES:]
        for back in range(1, DN_CONV):
            acc = acc + cw[DN_CONV - 1 - back:DN_CONV - back, cols] * pltpu.roll(x, back, 0)[SUBLANES:]
        act_ref[:, cols] = _silu(acc)
    ext_ref[0:SUBLANES, :] = ext_ref[tm:tm + SUBLANES, :]

    ba = _dot_nt(h_ref[...], wba16_ref[...])
    lane = lax.broadcasted_iota(jnp.int32, ba.shape, 1)
    beta = _sigmoid(ba)
    g = -jnp.exp(alog_ref[...]) * _softplus(ba + dtb_ref[...])
    gate_ref[...] = jnp.where(lane < DN_HEADS, beta, g)

    ri = lax.broadcasted_iota(jnp.int32, (C, 2 * C), 0)
    cj = lax.broadcasted_iota(jnp.int32, (C, 2 * C), 1)
    right = cj >= C
    ci = jnp.where(right, cj - C, cj)
    tril = ri >= ci
    strict = ri > ci
    eye_f = (ri == ci).astype(F32)
    blk = DN_SOLVE_BLOCK
    same_blk = (ri // blk) == (ci // blk)
    sub_blk = ((ri // blk) == (ci // blk) + 1) & ((ri // (2 * blk)) == (ci // (2 * blk)))
    low_half = (ri >= C // 2) & (ci < C // 2)
    right_row = right[0:1, :]
    rs = lax.broadcasted_iota(jnp.int32, (C, C), 0)
    cs = lax.broadcasted_iota(jnp.int32, (C, C), 1)
    tril_f = (rs >= cs).astype(F32)
    triu2_f = (ri <= ci).astype(F32)
    nw = nw_ref[...]
    heads = range(DN_HEADS)
    pairs = range(DN_HEADS // 2)

    def bd(x):
        z = jnp.zeros_like(x)
        return jnp.concatenate([jnp.where(right, z, x), jnp.where(right, x, z)], axis=0)

    def bd_rows(a, b):
        za = jnp.zeros_like(a)
        return jnp.concatenate([jnp.concatenate([a, za], axis=1), jnp.concatenate([za, b], axis=1)], axis=0)

    def split(x):
        hi = x.astype(BF16)
        return hi, (x - hi.astype(F32)).astype(BF16)

    def dot3(a, b):
        (ah, al), (bh, bl) = a, b
        bdh = bd(bh)
        return (jnp.dot(ah, bdh, preferred_element_type=F32)
                + jnp.dot(al, bdh, preferred_element_type=F32)
                + jnp.dot(ah, bd(bl), preferred_element_type=F32))

    def bdot(a, b):
        return jnp.dot(a.astype(BF16), bd(b.astype(BF16)), preferred_element_type=F32)

    chunks = range(tm // C)
    hu = [(c, h) for c in chunks for h in heads]
    pu = [(c, p) for c in chunks for p in pairs]
    gt = {c: gate_ref[c * C:(c + 1) * C, :] for c in chunks}
    gc = {c: jnp.dot(tril_f, gt[c], preferred_element_type=F32, precision=HI) for c in chunks}
    gc_t = {c: _dot_tn(gt[c], triu2_f, precision=HI) for c in chunks}
    q, k, kb, vb, gcol, glast, eg = {}, {}, {}, {}, {}, {}, {}
    for c, h in hu:
        rows = slice(c * C, (c + 1) * C)
        lo = h * DN_DK
        qh = act_ref[rows, lo:lo + DN_DK]
        kh = act_ref[rows, W + lo:W + lo + DN_DK]
        vh = act_ref[rows, 2 * W + lo:2 * W + lo + DN_DV]
        q[c, h] = qh * lax.rsqrt(jnp.sum(qh * qh, axis=-1, keepdims=True) + NORM_EPS) * (DN_DK ** -0.5)
        kh = kh * lax.rsqrt(jnp.sum(kh * kh, axis=-1, keepdims=True) + NORM_EPS)
        k[c, h] = kh
        bcol = gt[c][:, h:h + 1]
        gc_h = gc[c][:, DN_HEADS + h:DN_HEADS + h + 1]
        gcol[c, h] = gc_h
        glast[c, h] = gc_h[C - 1:C, :]
        eg[c, h] = jnp.exp(gc_h)
        kb[c, h] = kh * bcol
        vb[c, h] = vh * bcol
    k16 = {u: k[u].astype(BF16) for u in hu}
    kbd = {(c, p): bd_rows(k16[c, 2 * p], k16[c, 2 * p + 1]) for c, p in pu}
    kb2 = {(c, p): jnp.concatenate([kb[c, 2 * p], kb[c, 2 * p + 1]], axis=1).astype(BF16) for c, p in pu}
    q2 = {(c, p): jnp.concatenate([q[c, 2 * p], q[c, 2 * p + 1]], axis=1).astype(BF16) for c, p in pu}
    kk = {u: _dot_nt(kb2[u], kbd[u]) for u in pu}
    qk = {u: _dot_nt(q2[u], kbd[u]) for u in pu}
    dec = {}
    for c, p in pu:
        a, b = 2 * p, 2 * p + 1
        gcol2 = jnp.where(right, gcol[c, b], gcol[c, a])
        grow2 = jnp.where(right_row, gc_t[c][DN_HEADS + b:DN_HEADS + b + 1, :],
                          gc_t[c][DN_HEADS + a:DN_HEADS + a + 1, :])
        dec[c, p] = jnp.where(tril, jnp.exp(jnp.where(tril, gcol2 - grow2, 0.0)), 0.0)
    lmat = {u: jnp.where(strict, kk[u] * dec[u], 0.0) for u in pu}
    attn = {u: (qk[u] * dec[u]).astype(BF16) for u in pu}

    pw = {u: jnp.where(same_blk, -lmat[u], 0.0) for u in pu}
    tinv = {u: eye_f + pw[u] for u in pu}
    pws = {u: split(pw[u]) for u in pu}
    for _ in range(int(math.log2(blk)) - 1):
        pw = {u: dot3(pws[u], pws[u]) for u in pu}
        pws = {u: split(pw[u]) for u in pu}
        tinv = {u: tinv[u] + dot3(split(tinv[u]), pws[u]) for u in pu}
    x1 = {u: bdot(jnp.where(sub_blk, lmat[u], 0.0), tinv[u]) for u in pu}
    tinv = {u: tinv[u] - bdot(tinv[u], x1[u]) for u in pu}
    x2 = {u: bdot(jnp.where(low_half, lmat[u], 0.0), tinv[u]) for u in pu}
    tinv = {u: tinv[u] - bdot(tinv[u], x2[u]) for u in pu}

    rhs = {u: jnp.concatenate([vb[u], kb[u] * eg[u]], axis=1).astype(BF16) for u in hu}
    sol2 = {(c, p): jnp.dot(tinv[c, p].astype(BF16), bd_rows(rhs[c, 2 * p], rhs[c, 2 * p + 1]),
                            preferred_element_type=F32) for c, p in pu}
    qe = {u: q[u] * eg[u] for u in hu}
    kd = {u: (k[u] * jnp.exp(glast[u] - gcol[u])).astype(BF16) for u in hu}

    s = [s_ref[h] for h in heads]
    for c in chunks:
        rows = slice(c * C, (c + 1) * C)
        u_c = [sol2[c, h // 2][:, (h % 2) * 2 * DN_DV:(h % 2) * 2 * DN_DV + DN_DV] for h in heads]
        w_c = [sol2[c, h // 2][:, (h % 2) * 2 * DN_DV + DN_DV:(h % 2 + 1) * 2 * DN_DV] for h in heads]
        s16 = [s[h].astype(BF16) for h in heads]
        wq = [jnp.concatenate([w_c[h], qe[c, h]], axis=0).astype(BF16) for h in heads]
        wqs = [jnp.dot(wq[h], s16[h], preferred_element_type=F32) for h in heads]
        vn16 = [(u_c[h] - wqs[h][:C, :]).astype(BF16) for h in heads]
        av2 = [jnp.dot(attn[c, p], bd_rows(vn16[2 * p], vn16[2 * p + 1]), preferred_element_type=F32)
               for p in pairs]
        s = [s[h] * jnp.exp(glast[c, h]) + _dot_tn(kd[c, h], vn16[h]) for h in heads]
        for h in heads:
            lo = h * DN_DV
            zh = z_ref[rows, lo:lo + DN_DV]
            oh = wqs[h][C:, :] + av2[h // 2][:, (h % 2) * DN_DV:(h % 2 + 1) * DN_DV]
            on = oh * lax.rsqrt(jnp.mean(oh * oh, axis=-1, keepdims=True) + NORM_EPS) * nw
            o_ref[rows, lo:lo + DN_DV] = (on * _silu(zh)).astype(o_ref.dtype)
    for h in heads:
        s_ref[h] = s[h]


def deltanet(proj, h, w_in_t, layer, conv_w, a_log, dt_bias, norm_w, tm=256):
    L = proj.shape[0]
    D = h.shape[1]
    W = DN_QK_WIDTH
    pad = jnp.zeros((1, LANES), F32)
    alog_row = pad.at[0, DN_HEADS:2 * DN_HEADS].set(a_log)
    dtb_row = pad.at[0, DN_HEADS:2 * DN_HEADS].set(dt_bias)
    kern = functools.partial(_dn_kernel, tm=tm)
    return pl.pallas_call(
        kern,
        grid=(L // tm,),
        in_specs=[pl.BlockSpec((tm, W), lambda t: (t, 1)),
                  pl.BlockSpec((tm, W), lambda t: (t, 2)),
                  pl.BlockSpec((tm, W), lambda t: (t, 3)),
                  pl.BlockSpec((tm, W), lambda t: (t, 4)),
                  pl.BlockSpec((tm, D), lambda t: (t, 0)),
                  pl.BlockSpec((None, pl.Element(2 * DN_HEADS), pl.Element(D)), lambda t: (layer, OFF_Z, 0)),
                  pl.BlockSpec((DN_CONV, 3 * W), lambda t: (0, 0)),
                  pl.BlockSpec((1, LANES), lambda t: (0, 0)),
                  pl.BlockSpec((1, LANES), lambda t: (0, 0)),
                  pl.BlockSpec((1, DN_DV), lambda t: (0, 0))],
        out_specs=pl.BlockSpec((tm, DN_V_WIDTH), lambda t: (t, 0)),
        out_shape=jax.ShapeDtypeStruct((L, DN_V_WIDTH), BF16),
        scratch_shapes=[pltpu.VMEM((tm + SUBLANES, 3 * W), F32),
                        pltpu.VMEM((tm, 3 * W), F32),
                        pltpu.VMEM((tm, LANES), F32),
                        pltpu.VMEM((DN_HEADS, DN_DK, DN_DV), F32),
                        pltpu.VMEM((LANES, D), BF16)],
        compiler_params=_cparams(("arbitrary",)),
        name="deltanet",
    )(proj, proj, proj, proj, h, w_in_t, conv_w, alog_row, dtb_row, norm_w.reshape(1, DN_DV))


def _merge_kernel(ys_ref, yd_ref, wga_ref, wgb_ref, wdn_ref, gs_ref, gd_ref, o_ref,
                  wga16_ref, wgb16_ref, wdn16_ref):
    @pl.when(pl.program_id(1) == 0)
    def _():
        wga16_ref[...] = wga_ref[...].astype(BF16)
        wgb16_ref[...] = wgb_ref[...].astype(BF16)
        wdn16_ref[...] = wdn_ref[...].astype(BF16)

    ys = ys_ref[...]
    ga = jnp.dot(ys, wga16_ref[...], preferred_element_type=F32)
    gb = jnp.dot(ys, wgb16_ref[...], preferred_element_type=F32)
    dn = jnp.dot(yd_ref[...], wdn16_ref[...], preferred_element_type=F32)
    br_s5 = ga * _sigmoid(gb)
    o_ref[...] = (_sigmoid(gs_ref[...]) * br_s5 + _sigmoid(gd_ref[...]) * dn).astype(o_ref.dtype)


def merge(y_s5, y_dn, glu_w, dn_w, proj, layer, tm=1024, tn=512):
    L = y_s5.shape[0]
    nb = D_MODEL // tn
    gs_blk = OFF_Z // tn
    gd_blk = gs_blk + nb
    return pl.pallas_call(
        _merge_kernel,
        grid=(nb, L // tm),
        in_specs=[pl.BlockSpec((tm, S5_WIDTH), lambda n, m: (m, 0)),
                  pl.BlockSpec((tm, DN_V_WIDTH), lambda n, m: (m, 0)),
                  pl.BlockSpec((None, S5_WIDTH, tn), lambda n, m: (layer, 0, n)),
                  pl.BlockSpec((None, S5_WIDTH, tn), lambda n, m: (layer, 0, n + nb)),
                  pl.BlockSpec((None, DN_V_WIDTH, tn), lambda n, m: (layer, 0, n)),
                  pl.BlockSpec((tm, tn), lambda n, m: (m, n + gs_blk)),
                  pl.BlockSpec((tm, tn), lambda n, m: (m, n + gd_blk))],
        out_specs=pl.BlockSpec((tm, tn), lambda n, m: (m, n)),
        out_shape=jax.ShapeDtypeStruct((L, D_MODEL), BF16),
        scratch_shapes=[pltpu.VMEM((S5_WIDTH, tn), BF16),
                        pltpu.VMEM((S5_WIDTH, tn), BF16),
                        pltpu.VMEM((DN_V_WIDTH, tn), BF16)],
        compiler_params=_cparams(("parallel", "arbitrary")),
        name="merge",
    )(y_s5, y_dn, glu_w, glu_w, dn_w, proj, proj)


def _out_proj_kernel(m_ref, w_ref, x_ref, nw_ref, xo_ref, h_ref, w16_ref):
    @pl.when(pl.program_id(0) == 0)
    def _():
        w16_ref[...] = w_ref[...].astype(BF16)

    xo = x_ref[...] + jnp.dot(m_ref[...], w16_ref[...], preferred_element_type=F32)
    xo_ref[...] = xo
    h_ref[...] = _rms_rows(xo, nw_ref[...]).astype(h_ref.dtype)


def out_proj(merged, w_out, x, norm_w, layer, tm=512):
    L, D = x.shape
    return pl.pallas_call(
        _out_proj_kernel,
        grid=(L // tm,),
        in_specs=[pl.BlockSpec((tm, D), lambda m: (m, 0)),
                  pl.BlockSpec((None, D, D), lambda m: (layer, 0, 0), pipeline_mode=pl.Buffered(1)),
                  pl.BlockSpec((tm, D), lambda m: (m, 0)),
                  pl.BlockSpec((1, D), lambda m: (0, 0))],
        out_specs=[pl.BlockSpec((tm, D), lambda m: (m, 0)),
                   pl.BlockSpec((tm, D), lambda m: (m, 0))],
        out_shape=[jax.ShapeDtypeStruct((L, D), F32),
                   jax.ShapeDtypeStruct((L, D), BF16)],
        scratch_shapes=[pltpu.VMEM((D, D), BF16)],
        compiler_params=_cparams(("arbitrary",)),
        name="out_proj",
    )(merged, w_out, x, norm_w.reshape(1, D))


def _ffn_up_kernel(h_ref, wa_ref, wv_ref, cwa_ref, cwv_ref, wd_ref, o_ref, wd16_ref, ext_ref, w16_ref,
                   *, tm, tn):
    @pl.when(pl.program_id(1) == 0)
    def _():
        ext_ref[...] = jnp.zeros_like(ext_ref)
        w16_ref[:, :tn] = wa_ref[...].astype(BF16)
        w16_ref[:, tn:] = wv_ref[...].astype(BF16)

    wd16_ref[...] = wd_ref[...].astype(BF16)

    up = jnp.dot(h_ref[...], w16_ref[...], preferred_element_type=F32)
    cw = jnp.concatenate([cwa_ref[...], cwv_ref[...]], axis=1)
    prev = ext_ref[...]
    r8 = lax.broadcasted_iota(jnp.int32, prev.shape, 0)
    acc = cw[FFN_CONV - 1:FFN_CONV, :] * up
    for back in range(1, FFN_CONV):
        sh = pltpu.roll(up, back, 0)
        head = jnp.where(r8 < back, pltpu.roll(prev, back, 0), sh[:SUBLANES])
        sh = jnp.concatenate([head, sh[SUBLANES:]], axis=0)
        acc = acc + cw[FFN_CONV - 1 - back:FFN_CONV - back, :] * sh
    o_ref[...] = (_silu(acc[:, :tn]) * acc[:, tn:]).astype(o_ref.dtype)
    ext_ref[...] = up[tm - SUBLANES:, :]


def ffn_up_call(h, w_up, conv_w, w_down, layer, tm=1024, tn=512):
    L, D = h.shape
    nb = FFN_DIM // tn
    nm = L // tm
    slab = FFN_DIM // (nb * nm)
    assert slab * nb * nm == FFN_DIM and slab % (2 * SUBLANES) == 0
    kern = functools.partial(_ffn_up_kernel, tm=tm, tn=tn)
    return pl.pallas_call(
        kern,
        grid=(nb, nm),
        in_specs=[pl.BlockSpec((tm, D), lambda n, m: (m, 0)),
                  pl.BlockSpec((None, D, tn), lambda n, m: (layer, 0, n)),
                  pl.BlockSpec((None, D, tn), lambda n, m: (layer, 0, n + nb)),
                  pl.BlockSpec((FFN_CONV, tn), lambda n, m: (0, n)),
                  pl.BlockSpec((FFN_CONV, tn), lambda n, m: (0, n + nb)),
                  pl.BlockSpec((None, slab, D), lambda n, m: (layer, n * nm + m, 0))],
        out_specs=[pl.BlockSpec((tm, tn), lambda n, m: (m, n)),
                   pl.BlockSpec((slab, D), lambda n, m: (n * nm + m, 0))],
        out_shape=[jax.ShapeDtypeStruct((L, FFN_DIM), BF16),
                   jax.ShapeDtypeStruct((FFN_DIM, D), BF16)],
        scratch_shapes=[pltpu.VMEM((SUBLANES, 2 * tn), F32),
                        pltpu.VMEM((D, 2 * tn), BF16)],
        compiler_params=_cparams(("parallel", "arbitrary")),
        name="ffn_up",
    )(h, w_up, w_up, conv_w, conv_w, w_down)


def _ffn_down_kernel(a_ref, w_ref, x_ref, nw_ref, *refs, last):
    acc_ref = refs[0]
    k = pl.program_id(1)

    @pl.when(k == 0)
    def _():
        acc_ref[...] = x_ref[...]

    acc_ref[...] += jnp.dot(a_ref[...], w_ref[...], preferred_element_type=F32)

    @pl.when(k == pl.num_programs(1) - 1)
    def _():
        xo = acc_ref[...]
        if last:
            acc_ref[...] = _rms_rows(xo, nw_ref[...])
        else:
            refs[1][...] = _rms_rows(xo, nw_ref[...]).astype(refs[1].dtype)


def ffn_down_call(a, w_down, x, norm_w, last, tm=1024, tk=512):
    L, D = x.shape
    K = a.shape[1]
    row = pl.BlockSpec((tm, D), lambda m, k: (m, 0))
    if last:
        out_specs = [row]
        out_shape = [jax.ShapeDtypeStruct((L, D), F32)]
    else:
        out_specs = [row, row]
        out_shape = [jax.ShapeDtypeStruct((L, D), F32), jax.ShapeDtypeStruct((L, D), BF16)]
    kern = functools.partial(_ffn_down_kernel, last=last)
    return pl.pallas_call(
        kern,
        grid=(L // tm, K // tk),
        in_specs=[pl.BlockSpec((tm, tk), lambda m, k: (m, k)),
                  pl.BlockSpec((tk, D), lambda m, k: (k, 0)),
                  row,
                  pl.BlockSpec((1, D), lambda m, k: (0, 0))],
        out_specs=out_specs,
        out_shape=out_shape,
        compiler_params=_cparams(("parallel", "arbitrary")),
        name="ffn_down",
    )(a, w_down, x, norm_w.reshape(1, D))


def kernel(x, mix_norm_w, w_in, s5_log_dt, s5_a_re, s5_a_im, s5_b_re, s5_b_im, s5_c_re, s5_c_im, s5_d, s5_glu_w, dn_conv_w, dn_a_log, dn_dt_bias, dn_norm_w, dn_proj_w, w_out, ffn_norm_w, ffn_up, ffn_conv_w, ffn_down, final_norm_w):
    B, L, D = x.shape
    depth = w_in.shape[0]
    assert B == 1 and D == D_MODEL
    xs = x.reshape(L, D)
    h = rmsnorm_cast(xs, mix_norm_w[0])
    abr, abi, bbr, bbi = s5_discretise(s5_log_dt, s5_a_re, s5_a_im, s5_b_re, s5_b_im)
    s5_ar, s5_ai, s5_wb, s5_wc = s5_layout_params(abr, abi, bbr, bbi, s5_c_re, s5_c_im, depth)
    s5_ti = 32
    perm, perm_t = s5_perm(s5_ti)
    w_in_t = jnp.swapaxes(w_in, 1, 2)
    for l in range(depth):
        proj = in_proj(h, w_in_t, l)

        y_s5 = s5_scan(proj, perm, perm_t, s5_ar, s5_ai, s5_wb, s5_wc, s5_d, l, ti=s5_ti)
        y_dn = deltanet(proj, h, w_in_t, l, dn_conv_w[l], dn_a_log[l], dn_dt_bias[l], dn_norm_w[l])

        merged = merge(y_s5, y_dn, s5_glu_w, dn_proj_w, proj, l)
        xs, h = out_proj(merged, w_out, xs, ffn_norm_w[l], l)
        act, w_down = ffn_up_call(h, ffn_up, ffn_conv_w[l], ffn_down, l)
        if l + 1 < depth:
            xs, h = ffn_down_call(act, w_down, xs, mix_norm_w[l + 1], last=False)
        else:
            (out,) = ffn_down_call(act, w_down, xs, final_norm_w, last=True)
    return out.reshape(B, L, D)
```

```python
import functools
import math

import jax
import jax.numpy as jnp
from jax import lax
from jax.experimental import pallas as pl
from jax.experimental.pallas import tpu as pltpu

F32 = jnp.float32
BF16 = jnp.bfloat16

D_MODEL = 2048
S5_WIDTH = 1024
S5_GROUP = 16
S5_GROUPS = 64
S5_STATE = 64
DN_HEADS = 8
DN_DK = 128
DN_DV = 128
DN_QK_WIDTH = DN_HEADS * DN_DK
DN_V_WIDTH = DN_HEADS * DN_DV
DN_CONV = 4
DN_CHUNK = 64
DN_SOLVE_BLOCK = 16
FFN_DIM = 5632
FFN_CONV = 3
NORM_EPS = 1e-6

OFF_U = S5_WIDTH
OFF_QKV = OFF_U + 2 * DN_QK_WIDTH + DN_V_WIDTH
OFF_Z = OFF_QKV + DN_V_WIDTH
OFF_BETA = OFF_Z + DN_HEADS
OFF_ALPHA = OFF_BETA + DN_HEADS
OFF_GS = OFF_ALPHA + D_MODEL
N_IN = OFF_GS + D_MODEL
N_MAIN = N_IN - 2 * DN_HEADS

LANES = 128
SUBLANES = 8
VMEM_LIMIT = 56 * 1024 * 1024

S5_SEGS = SUBLANES
S5_GB = 8
S5_NBLK = S5_GROUPS // S5_GB
S5_BLK_CH = S5_GB * S5_GROUP
S5_BLK_ST = S5_GB * S5_STATE

HI = lax.Precision.HIGHEST


def _cparams(sem):
    return pltpu.CompilerParams(dimension_semantics=sem, vmem_limit_bytes=VMEM_LIMIT)


def _sigmoid(x):
    return 1.0 / (1.0 + jnp.exp(-x))


def _silu(x):
    hx = 0.5 * x
    return hx + hx * jnp.tanh(hx)


def _softplus(x):
    return jnp.maximum(x, 0.0) + jnp.log1p(jnp.exp(-jnp.abs(x)))


def _rms_rows(x, w):
    ms = jnp.mean(x * x, axis=-1, keepdims=True)
    return x * lax.rsqrt(ms + NORM_EPS) * w


def _rmsnorm_kernel(x_ref, w_ref, o_ref):
    o_ref[...] = _rms_rows(x_ref[...], w_ref[...]).astype(o_ref.dtype)


def rmsnorm_cast(x, w, tm=512):
    L, D = x.shape
    return pl.pallas_call(
        _rmsnorm_kernel,
        grid=(L // tm,),
        in_specs=[pl.BlockSpec((tm, D), lambda m: (m, 0)),
                  pl.BlockSpec((1, D), lambda m: (0, 0))],
        out_specs=pl.BlockSpec((tm, D), lambda m: (m, 0)),
        out_shape=jax.ShapeDtypeStruct((L, D), BF16),
        compiler_params=_cparams(("parallel",)),
        name="rmsnorm",
    )(x, w.reshape(1, D))


def _dot_nt(a, b):
    return lax.dot_general(a, b, (((1,), (1,)), ((), ())), preferred_element_type=F32)


def _in_proj_kernel(h_ref, wt_ref, o_ref, w16_ref):
    @pl.when(pl.program_id(1) == 0)
    def _():
        w16_ref[...] = wt_ref[...].astype(BF16)

    o_ref[...] = _dot_nt(h_ref[...], w16_ref[...])


def in_proj(h, w_in_t, layer, tm=1024, tn=1024):
    L, D = h.shape
    n_mix = OFF_Z // tn

    def w_index(n, m):
        row = n * tn + jnp.where(n >= n_mix, OFF_ALPHA - OFF_Z, 0)
        return layer, pl.multiple_of(row, 2 * DN_HEADS), 0

    return pl.pallas_call(
        _in_proj_kernel,
        grid=(N_MAIN // tn, L // tm),
        in_specs=[pl.BlockSpec((tm, D), lambda n, m: (m, 0)),
                  pl.BlockSpec((None, pl.Element(tn), pl.Element(D)), w_index)],
        out_specs=pl.BlockSpec((tm, tn), lambda n, m: (m, n)),
        out_shape=jax.ShapeDtypeStruct((L, N_MAIN), F32),
        scratch_shapes=[pltpu.VMEM((tn, D), BF16)],
        compiler_params=_cparams(("parallel", "arbitrary")),
        name="in_proj",
    )(h, w_in_t)


def _s5_disc_kernel(ldt_ref, are_ref, aim_ref, bre_ref, bim_ref,
                    abr_ref, abi_ref, bbr_ref, bbi_ref):
    lr = are_ref[...]
    li = aim_ref[...]
    dt = jnp.exp(ldt_ref[...])
    mag = jnp.exp(lr * dt)
    abar_re = mag * jnp.cos(li * dt)
    abar_im = mag * jnp.sin(li * dt)
    den = lr * lr + li * li
    nr = abar_re - 1.0
    ni = abar_im
    coef_re = (nr * lr + ni * li) / den
    coef_im = (ni * lr - nr * li) / den
    br = bre_ref[...]
    bi = bim_ref[...]
    abr_ref[...] = abar_re
    abi_ref[...] = abar_im
    bbr_ref[...] = coef_re * br - coef_im * bi
    bbi_ref[...] = coef_re * bi + coef_im * br


def s5_discretise(log_dt, a_re, a_im, b_re, b_im):
    P, HG = S5_STATE, S5_GROUP
    dg = log_dt.size
    bt_re = jnp.transpose(b_re.reshape(dg, P, HG), (0, 2, 1))
    bt_im = jnp.transpose(b_im.reshape(dg, P, HG), (0, 2, 1))
    return pl.pallas_call(
        _s5_disc_kernel,
        out_shape=[jax.ShapeDtypeStruct((dg, 1, P), F32),
                   jax.ShapeDtypeStruct((dg, 1, P), F32),
                   jax.ShapeDtypeStruct((dg, HG, P), F32),
                   jax.ShapeDtypeStruct((dg, HG, P), F32)],
        name="s5_disc",
    )(jnp.broadcast_to(log_dt.reshape(dg, 1, 1), (dg, 1, P)), a_re.reshape(dg, 1, P), a_im.reshape(dg, 1, P),
      bt_re, bt_im)


def _block_diag(t, rows_per_blk, cols_per_blk):
    tiled = jnp.tile(t, (1,) * (t.ndim - 1) + (S5_GB,))
    r = lax.broadcasted_iota(jnp.int32, tiled.shape, t.ndim - 2) // rows_per_blk
    c = lax.broadcasted_iota(jnp.int32, tiled.shape, t.ndim - 1) // cols_per_blk
    return jnp.where(r == c, tiled, jnp.zeros_like(tiled))


def s5_layout_params(abar_re, abar_im, bbar_re, bbar_im, c_re, c_im, depth):
    P, HG = S5_STATE, S5_GROUP
    ns = S5_GROUPS * P
    a_r = jnp.broadcast_to(abar_re.reshape(depth, 1, ns), (depth, S5_SEGS, ns))
    a_i = jnp.broadcast_to(abar_im.reshape(depth, 1, ns), (depth, S5_SEGS, ns))
    bb_re = bbar_re.reshape(depth, S5_NBLK, S5_BLK_CH, P)
    bb_im = bbar_im.reshape(depth, S5_NBLK, S5_BLK_CH, P)
    wb = jnp.concatenate([_block_diag(bb_re, HG, P), _block_diag(bb_im, HG, P)], axis=-1).astype(BF16)
    ct_re = jnp.transpose(c_re, (0, 1, 3, 2)).reshape(depth, S5_NBLK, S5_BLK_ST, HG)
    ct_im = jnp.transpose(c_im, (0, 1, 3, 2)).reshape(depth, S5_NBLK, S5_BLK_ST, HG)
    wc = jnp.concatenate([_block_diag(ct_re, P, HG), -_block_diag(ct_im, P, HG)], axis=-2).astype(BF16)
    return a_r, a_i, wb, wc


def _s5_kernel(u_ref, p_ref, pt_ref, wb_ref, ar_ref, ai_ref, wc_ref, d_ref, y_ref,
               bu_ref, x_ref, sr_ref, si_ref, *, ti, seg_len):
    R = S5_SEGS * ti
    NS = S5_BLK_ST
    ph = pl.program_id(0)
    it = pl.program_id(1)

    @pl.when(jnp.logical_and(ph == 0, it == 0))
    def _():
        sr_ref[...] = jnp.zeros_like(sr_ref)
        si_ref[...] = jnp.zeros_like(si_ref)

    @pl.when(jnp.logical_and(ph == 1, it == 0))
    def _():
        pr, pi = ar_ref[...], ai_ref[...]
        for _ in range(int(math.log2(seg_len))):
            pr, pi = pr * pr - pi * pi, 2.0 * pr * pi
        er, ei = sr_ref[...], si_ref[...]
        row = lax.broadcasted_iota(jnp.int32, er.shape, 0)
        zr = jnp.zeros_like(er)
        zi = jnp.zeros_like(ei)
        for _ in range(S5_SEGS - 1):
            nr = pr * zr - pi * zi + er
            ni = pr * zi + pi * zr + ei
            zr = jnp.where(row >= 1, pltpu.roll(nr, 1, 0), 0.0)
            zi = jnp.where(row >= 1, pltpu.roll(ni, 1, 0), 0.0)
        sr_ref[...] = zr
        si_ref[...] = zi

    u_nat = u_ref[...].reshape(R, S5_WIDTH)
    u2 = jnp.dot(p_ref[...], u_nat.astype(BF16), preferred_element_type=F32).astype(BF16)
    for b in range(S5_NBLK):
        bu_ref[:, b * 2 * NS:(b + 1) * 2 * NS] = jnp.dot(
            u2[:, b * S5_BLK_CH:(b + 1) * S5_BLK_CH], wb_ref[b], preferred_element_type=F32)

    def scan(store):
        for b in range(S5_NBLK):
            ar = ar_ref[:, b * NS:(b + 1) * NS]
            ai = ai_ref[:, b * NS:(b + 1) * NS]
            c_re = b * 2 * NS
            c_im = c_re + NS

            def body(i, carry):
                xr, xi = carry
                r0 = pl.multiple_of(i * SUBLANES, SUBLANES)
                br = bu_ref[pl.ds(r0, SUBLANES), c_re:c_re + NS]
                bi = bu_ref[pl.ds(r0, SUBLANES), c_im:c_im + NS]
                nr = ar * xr - ai * xi + br
                ni = ar * xi + ai * xr + bi
                if store:
                    x_ref[pl.ds(r0, SUBLANES), c_re:c_re + NS] = nr
                    x_ref[pl.ds(r0, SUBLANES), c_im:c_im + NS] = ni
                return nr, ni

            xr, xi = lax.fori_loop(
                0, ti, body, (sr_ref[:, b * NS:(b + 1) * NS], si_ref[:, b * NS:(b + 1) * NS]), unroll=8)
            sr_ref[:, b * NS:(b + 1) * NS] = xr
            si_ref[:, b * NS:(b + 1) * NS] = xi

    @pl.when(ph == 0)
    def _():
        scan(False)

    @pl.when(ph == 1)
    def _():
        scan(True)
        ys = []
        for b in range(S5_NBLK):
            xb = x_ref[:, b * 2 * NS:(b + 1) * 2 * NS].astype(BF16)
            ys.append(jnp.dot(xb, wc_ref[b], preferred_element_type=F32))
        y2 = jnp.concatenate(ys, axis=1)
        y_hi = y2.astype(BF16)
        y_lo = (y2 - y_hi.astype(F32)).astype(BF16)
        pt = pt_ref[...]
        y_nat = (jnp.dot(pt, y_hi, preferred_element_type=F32)
                 + jnp.dot(pt, y_lo, preferred_element_type=F32))
        y_nat = y_nat + d_ref[...] * u_nat
        y = 0.5 * y_nat * (1.0 + lax.erf(y_nat * (1.0 / math.sqrt(2.0))))
        y_ref[...] = y.reshape(S5_SEGS, ti, S5_WIDTH).astype(y_ref.dtype)


def s5_perm(ti):
    R = S5_SEGS * ti
    r = lax.broadcasted_iota(jnp.int32, (R, R), 0)
    c = lax.broadcasted_iota(jnp.int32, (R, R), 1)
    perm = (c == (r % S5_SEGS) * ti + r // S5_SEGS).astype(BF16)
    return perm, perm.T


def s5_scan(proj, perm, perm_t, a_r, a_i, wb, wc, d, layer, ti=32):
    L = proj.shape[0]
    seg_len = L // S5_SEGS
    assert seg_len * S5_SEGS == L and seg_len % ti == 0 and (seg_len & (seg_len - 1)) == 0
    R = S5_SEGS * ti
    nt = seg_len // ti
    proj3 = proj.reshape(S5_SEGS, seg_len, proj.shape[1])
    ns = S5_GROUPS * S5_STATE
    kern = functools.partial(_s5_kernel, ti=ti, seg_len=seg_len)
    y = pl.pallas_call(
        kern,
        grid=(2, nt),
        in_specs=[pl.BlockSpec((S5_SEGS, ti, S5_WIDTH), lambda p, t: (0, t, 0)),
                  pl.BlockSpec((R, R), lambda p, t: (0, 0)),
                  pl.BlockSpec((R, R), lambda p, t: (0, 0)),
                  pl.BlockSpec((None, S5_NBLK, S5_BLK_CH, 2 * S5_BLK_ST), lambda p, t: (layer, 0, 0, 0)),
                  pl.BlockSpec((None, S5_SEGS, ns), lambda p, t: (layer, 0, 0)),
                  pl.BlockSpec((None, S5_SEGS, ns), lambda p, t: (layer, 0, 0)),
                  pl.BlockSpec((None, S5_NBLK, 2 * S5_BLK_ST, S5_BLK_CH), lambda p, t: (layer, 0, 0, 0)),
                  pl.BlockSpec((None, 1, S5_WIDTH), lambda p, t: (layer, 0, 0))],
        out_specs=pl.BlockSpec((S5_SEGS, ti, S5_WIDTH), lambda p, t: (0, t * p, 0)),
        out_shape=jax.ShapeDtypeStruct((S5_SEGS, seg_len, S5_WIDTH), BF16),
        scratch_shapes=[pltpu.VMEM((R, 2 * ns), F32),
                        pltpu.VMEM((R, 2 * ns), F32),
                        pltpu.VMEM((S5_SEGS, ns), F32),
                        pltpu.VMEM((S5_SEGS, ns), F32)],
        compiler_params=_cparams(("arbitrary", "arbitrary")),
        name="s5_scan",
    )(proj3, perm, perm_t, wb, a_r, a_i, wc, d.reshape(-1, 1, S5_WIDTH))
    return y.reshape(L, S5_WIDTH)


def _dot_tn(a, b, precision=None):
    return lax.dot_general(a, b, (((0,), (0,)), ((), ())), preferred_element_type=F32,
                           precision=precision)


def _dn_kernel(q_ref, k_ref, v_ref, z_ref, h_ref, wba_ref, cw_ref, alog_ref, dtb_ref, nw_ref, o_ref,
               ext_ref, act_ref, gate_ref, s_ref, wba16_ref, *, tm):
    C = DN_CHUNK
    W = DN_QK_WIDTH
    it = pl.program_id(0)

    @pl.when(it == 0)
    def _():
        ext_ref[0:SUBLANES, :] = jnp.zeros((SUBLANES, 3 * W), F32)
        s_ref[...] = jnp.zeros_like(s_ref)
        wba16_ref[...] = jnp.zeros_like(wba16_ref)
        wba16_ref[0:2 * DN_HEADS, :] = wba_ref[...].astype(BF16)

    ext_ref[SUBLANES:SUBLANES + tm, 0:W] = q_ref[...]
    ext_ref[SUBLANES:SUBLANES + tm, W:2 * W] = k_ref[...]
    ext_ref[SUBLANES:SUBLANES + tm, 2 * W:3 * W] = v_ref[...]
    cw = cw_ref[...]
    for cb in range(3 * W // LANES):
        cols = slice(cb * LANES, (cb + 1) * LANES)
        x = ext_ref[:, cols]
        acc = cw[DN_CONV - 1:DN_CONV, cols] * x[SUBLANES:]
        for back in range(1, DN_CONV):
            acc = acc + cw[DN_CONV - 1 - back:DN_CONV - back, cols] * pltpu.roll(x, back, 0)[SUBLANES:]
        act_ref[:, cols] = _silu(acc)
    ext_ref[0:SUBLANES, :] = ext_ref[tm:tm + SUBLANES, :]

    ba = _dot_nt(h_ref[...], wba16_ref[...])
    lane = lax.broadcasted_iota(jnp.int32, ba.shape, 1)
    beta = _sigmoid(ba)
    g = -jnp.exp(alog_ref[...]) * _softplus(ba + dtb_ref[...])
    gate_ref[...] = jnp.where(lane < DN_HEADS, beta, g)

    ri = lax.broadcasted_iota(jnp.int32, (C, 2 * C), 0)
    cj = lax.broadcasted_iota(jnp.int32, (C, 2 * C), 1)
    right = cj >= C
    ci = jnp.where(right, cj - C, cj)
    tril = ri >= ci
    strict = ri > ci
    eye_f = (ri == ci).astype(F32)
    blk = DN_SOLVE_BLOCK
    same_blk = (ri // blk) == (ci // blk)
    sub_blk = ((ri // blk) == (ci // blk) + 1) & ((ri // (2 * blk)) == (ci // (2 * blk)))
    low_half = (ri >= C // 2) & (ci < C // 2)
    right_row = right[0:1, :]
    rs = lax.broadcasted_iota(jnp.int32, (C, C), 0)
    cs = lax.broadcasted_iota(jnp.int32, (C, C), 1)
    tril_f = (rs >= cs).astype(F32)
    triu2_f = (ri <= ci).astype(F32)
    nw = nw_ref[...]
    heads = range(DN_HEADS)
    pairs = range(DN_HEADS // 2)

    def bd(x):
        z = jnp.zeros_like(x)
        return jnp.concatenate([jnp.where(right, z, x), jnp.where(right, x, z)], axis=0)

    def bd_rows(a, b):
        za = jnp.zeros_like(a)
        return jnp.concatenate([jnp.concatenate([a, za], axis=1), jnp.concatenate([za, b], axis=1)], axis=0)

    def split(x):
        hi = x.astype(BF16)
        return hi, (x - hi.astype(F32)).astype(BF16)

    def dot3(a, b):
        (ah, al), (bh, bl) = a, b
        bdh = bd(bh)
        return (jnp.dot(ah, bdh, preferred_element_type=F32)
                + jnp.dot(al, bdh, preferred_element_type=F32)
                + jnp.dot(ah, bd(bl), preferred_element_type=F32))

    def bdot(a, b):
        return jnp.dot(a.astype(BF16), bd(b.astype(BF16)), preferred_element_type=F32)

    chunks = range(tm // C)
    hu = [(c, h) for c in chunks for h in heads]
    pu = [(c, p) for c in chunks for p in pairs]
    gt = {c: gate_ref[c * C:(c + 1) * C, :] for c in chunks}
    gc = {c: jnp.dot(tril_f, gt[c], preferred_element_type=F32, precision=HI) for c in chunks}
    gc_t = {c: _dot_tn(gt[c], triu2_f, precision=HI) for c in chunks}
    q, k, kb, vb, gcol, glast, eg = {}, {}, {}, {}, {}, {}, {}
    for c, h in hu:
        rows = slice(c * C, (c + 1) * C)
        lo = h * DN_DK
        qh = act_ref[rows, lo:lo + DN_DK]
        kh = act_ref[rows, W + lo:W + lo + DN_DK]
        vh = act_ref[rows, 2 * W + lo:2 * W + lo + DN_DV]
        q[c, h] = qh * lax.rsqrt(jnp.sum(qh * qh, axis=-1, keepdims=True) + NORM_EPS) * (DN_DK ** -0.5)
        kh = kh * lax.rsqrt(jnp.sum(kh * kh, axis=-1, keepdims=True) + NORM_EPS)
        k[c, h] = kh
        bcol = gt[c][:, h:h + 1]
        gc_h = gc[c][:, DN_HEADS + h:DN_HEADS + h + 1]
        gcol[c, h] = gc_h
        glast[c, h] = gc_h[C - 1:C, :]
        eg[c, h] = jnp.exp(gc_h)
        kb[c, h] = kh * bcol
        vb[c, h] = vh * bcol
    k16 = {u: k[u].astype(BF16) for u in hu}
    kbd = {(c, p): bd_rows(k16[c, 2 * p], k16[c, 2 * p + 1]) for c, p in pu}
    kb2 = {(c, p): jnp.concatenate([kb[c, 2 * p], kb[c, 2 * p + 1]], axis=1).astype(BF16) for c, p in pu}
    q2 = {(c, p): jnp.concatenate([q[c, 2 * p], q[c, 2 * p + 1]], axis=1).astype(BF16) for c, p in pu}
    kk = {u: _dot_nt(kb2[u], kbd[u]) for u in pu}
    qk = {u: _dot_nt(q2[u], kbd[u]) for u in pu}
    dec = {}
    for c, p in pu:
        a, b = 2 * p, 2 * p + 1
        gcol2 = jnp.where(right, gcol[c, b], gcol[c, a])
        grow2 = jnp.where(right_row, gc_t[c][DN_HEADS + b:DN_HEADS + b + 1, :],
                          gc_t[c][DN_HEADS + a:DN_HEADS + a + 1, :])
        dec[c, p] = jnp.where(tril, jnp.exp(jnp.where(tril, gcol2 - grow2, 0.0)), 0.0)
    lmat = {u: jnp.where(strict, kk[u] * dec[u], 0.0) for u in pu}
    attn = {u: (qk[u] * dec[u]).astype(BF16) for u in pu}

    pw = {u: jnp.where(same_blk, -lmat[u], 0.0) for u in pu}
    tinv = {u: eye_f + pw[u] for u in pu}
    pws = {u: split(pw[u]) for u in pu}
    for _ in range(int(math.log2(blk)) - 1):
        pw = {u: dot3(pws[u], pws[u]) for u in pu}
        pws = {u: split(pw[u]) for u in pu}
        tinv = {u: tinv[u] + dot3(split(tinv[u]), pws[u]) for u in pu}
    x1 = {u: bdot(jnp.where(sub_blk, lmat[u], 0.0), tinv[u]) for u in pu}
    tinv = {u: tinv[u] - bdot(tinv[u], x1[u]) for u in pu}
    x2 = {u: bdot(jnp.where(low_half, lmat[u], 0.0), tinv[u]) for u in pu}
    tinv = {u: tinv[u] - bdot(tinv[u], x2[u]) for u in pu}

    rhs = {u: jnp.concatenate([vb[u], kb[u] * eg[u]], axis=1).astype(BF16) for u in hu}
    sol2 = {(c, p): jnp.dot(tinv[c, p].astype(BF16), bd_rows(rhs[c, 2 * p], rhs[c, 2 * p + 1]),
                            preferred_element_type=F32) for c, p in pu}
    qe = {u: q[u] * eg[u] for u in hu}
    kd = {u: (k[u] * jnp.exp(glast[u] - gcol[u])).astype(BF16) for u in hu}

    s = [s_ref[h] for h in heads]
    for c in chunks:
        rows = slice(c * C, (c + 1) * C)
        u_c = [sol2[c, h // 2][:, (h % 2) * 2 * DN_DV:(h % 2) * 2 * DN_DV + DN_DV] for h in heads]
        w_c = [sol2[c, h // 2][:, (h % 2) * 2 * DN_DV + DN_DV:(h % 2 + 1) * 2 * DN_DV] for h in heads]
        s16 = [s[h].astype(BF16) for h in heads]
        wq = [jnp.concatenate([w_c[h], qe[c, h]], axis=0).astype(BF16) for h in heads]
        wqs = [jnp.dot(wq[h], s16[h], preferred_element_type=F32) for h in heads]
        vn16 = [(u_c[h] - wqs[h][:C, :]).astype(BF16) for h in heads]
        av2 = [jnp.dot(attn[c, p], bd_rows(vn16[2 * p], vn16[2 * p + 1]), preferred_element_type=F32)
               for p in pairs]
        s = [s[h] * jnp.exp(glast[c, h]) + _dot_tn(kd[c, h], vn16[h]) for h in heads]
        for h in heads:
            lo = h * DN_DV
            zh = z_ref[rows, lo:lo + DN_DV]
            oh = wqs[h][C:, :] + av2[h // 2][:, (h % 2) * DN_DV:(h % 2 + 1) * DN_DV]
            on = oh * lax.rsqrt(jnp.mean(oh * oh, axis=-1, keepdims=True) + NORM_EPS) * nw
            o_ref[rows, lo:lo + DN_DV] = (on * _silu(zh)).astype(o_ref.dtype)
    for h in heads:
        s_ref[h] = s[h]


def deltanet(proj, h, w_in_t, layer, conv_w, a_log, dt_bias, norm_w, tm=256):
    L = proj.shape[0]
    D = h.shape[1]
    W = DN_QK_WIDTH
    pad = jnp.zeros((1, LANES), F32)
    alog_row = pad.at[0, DN_HEADS:2 * DN_HEADS].set(a_log)
    dtb_row = pad.at[0, DN_HEADS:2 * DN_HEADS].set(dt_bias)
    kern = functools.partial(_dn_kernel, tm=tm)
    return pl.pallas_call(
        kern,
        grid=(L // tm,),
        in_specs=[pl.BlockSpec((tm, W), lambda t: (t, 1)),
                  pl.BlockSpec((tm, W), lambda t: (t, 2)),
                  pl.BlockSpec((tm, W), lambda t: (t, 3)),
                  pl.BlockSpec((tm, W), lambda t: (t, 4)),
                  pl.BlockSpec((tm, D), lambda t: (t, 0)),
                  pl.BlockSpec((None, pl.Element(2 * DN_HEADS), pl.Element(D)), lambda t: (layer, OFF_Z, 0)),
                  pl.BlockSpec((DN_CONV, 3 * W), lambda t: (0, 0)),
                  pl.BlockSpec((1, LANES), lambda t: (0, 0)),
                  pl.BlockSpec((1, LANES), lambda t: (0, 0)),
                  pl.BlockSpec((1, DN_DV), lambda t: (0, 0))],
        out_specs=pl.BlockSpec((tm, DN_V_WIDTH), lambda t: (t, 0)),
        out_shape=jax.ShapeDtypeStruct((L, DN_V_WIDTH), BF16),
        scratch_shapes=[pltpu.VMEM((tm + SUBLANES, 3 * W), F32),
                        pltpu.VMEM((tm, 3 * W), F32),
                        pltpu.VMEM((tm, LANES), F32),
                        pltpu.VMEM((DN_HEADS, DN_DK, DN_DV), F32),
                        pltpu.VMEM((LANES, D), BF16)],
        compiler_params=_cparams(("arbitrary",)),
        name="deltanet",
    )(proj, proj, proj, proj, h, w_in_t, conv_w, alog_row, dtb_row, norm_w.reshape(1, DN_DV))


def _merge_kernel(ys_ref, yd_ref, wga_ref, wgb_ref, wdn_ref, gs_ref, gd_ref, o_ref,
                  wga16_ref, wgb16_ref, wdn16_ref):
    @pl.when(pl.program_id(1) == 0)
    def _():
        wga16_ref[...] = wga_ref[...].astype(BF16)
        wgb16_ref[...] = wgb_ref[...].astype(BF16)
        wdn16_ref[...] = wdn_ref[...].astype(BF16)

    ys = ys_ref[...]
    ga = jnp.dot(ys, wga16_ref[...], preferred_element_type=F32)
    gb = jnp.dot(ys, wgb16_ref[...], preferred_element_type=F32)
    dn = jnp.dot(yd_ref[...], wdn16_ref[...], preferred_element_type=F32)
    br_s5 = ga * _sigmoid(gb)
    o_ref[...] = (_sigmoid(gs_ref[...]) * br_s5 + _sigmoid(gd_ref[...]) * dn).astype(o_ref.dtype)


def merge(y_s5, y_dn, glu_w, dn_w, proj, layer, tm=1024, tn=512):
    L = y_s5.shape[0]
    nb = D_MODEL // tn
    gs_blk = OFF_Z // tn
    gd_blk = gs_blk + nb
    return pl.pallas_call(
        _merge_kernel,
        grid=(nb, L // tm),
        in_specs=[pl.BlockSpec((tm, S5_WIDTH), lambda n, m: (m, 0)),
                  pl.BlockSpec((tm, DN_V_WIDTH), lambda n, m: (m, 0)),
                  pl.BlockSpec((None, S5_WIDTH, tn), lambda n, m: (layer, 0, n)),
                  pl.BlockSpec((None, S5_WIDTH, tn), lambda n, m: (layer, 0, n + nb)),
                  pl.BlockSpec((None, DN_V_WIDTH, tn), lambda n, m: (layer, 0, n)),
                  pl.BlockSpec((tm, tn), lambda n, m: (m, n + gs_blk)),
                  pl.BlockSpec((tm, tn), lambda n, m: (m, n + gd_blk))],
        out_specs=pl.BlockSpec((tm, tn), lambda n, m: (m, n)),
        out_shape=jax.ShapeDtypeStruct((L, D_MODEL), BF16),
        scratch_shapes=[pltpu.VMEM((S5_WIDTH, tn), BF16),
                        pltpu.VMEM((S5_WIDTH, tn), BF16),
                        pltpu.VMEM((DN_V_WIDTH, tn), BF16)],
        compiler_params=_cparams(("parallel", "arbitrary")),
        name="merge",
    )(y_s5, y_dn, glu_w, glu_w, dn_w, proj, proj)


def _out_proj_kernel(m_ref, w_ref, x_ref, nw_ref, xo_ref, h_ref, w16_ref):
    @pl.when(pl.program_id(0) == 0)
    def _():
        w16_ref[...] = w_ref[...].astype(BF16)

    xo = x_ref[...] + jnp.dot(m_ref[...], w16_ref[...], preferred_element_type=F32)
    xo_ref[...] = xo
    h_ref[...] = _rms_rows(xo, nw_ref[...]).astype(h_ref.dtype)


def out_proj(merged, w_out, x, norm_w, layer, tm=512):
    L, D = x.shape
    return pl.pallas_call(
        _out_proj_kernel,
        grid=(L // tm,),
        in_specs=[pl.BlockSpec((tm, D), lambda m: (m, 0)),
                  pl.BlockSpec((None, D, D), lambda m: (layer, 0, 0), pipeline_mode=pl.Buffered(1)),
                  pl.BlockSpec((tm, D), lambda m: (m, 0)),
                  pl.BlockSpec((1, D), lambda m: (0, 0))],
        out_specs=[pl.BlockSpec((tm, D), lambda m: (m, 0)),
                   pl.BlockSpec((tm, D), lambda m: (m, 0))],
        out_shape=[jax.ShapeDtypeStruct((L, D), F32),
                   jax.ShapeDtypeStruct((L, D), BF16)],
        scratch_shapes=[pltpu.VMEM((D, D), BF16)],
        compiler_params=_cparams(("arbitrary",)),
        name="out_proj",
    )(merged, w_out, x, norm_w.reshape(1, D))


def _ffn_up_kernel(h_ref, wa_ref, wv_ref, cwa_ref, cwv_ref, wd_ref, o_ref, wd16_ref, ext_ref, w16_ref,
                   *, tm, tn):
    @pl.when(pl.program_id(1) == 0)
    def _():
        ext_ref[...] = jnp.zeros_like(ext_ref)
        w16_ref[:, :tn] = wa_ref[...].astype(BF16)
        w16_ref[:, tn:] = wv_ref[...].astype(BF16)

    wd16_ref[...] = wd_ref[...].astype(BF16)

    up = jnp.dot(h_ref[...], w16_ref[...], preferred_element_type=F32)
    cw = jnp.concatenate([cwa_ref[...], cwv_ref[...]], axis=1)
    prev = ext_ref[...]
    r8 = lax.broadcasted_iota(jnp.int32, prev.shape, 0)
    acc = cw[FFN_CONV - 1:FFN_CONV, :] * up
    for back in range(1, FFN_CONV):
        sh = pltpu.roll(up, back, 0)
        head = jnp.where(r8 < back, pltpu.roll(prev, back, 0), sh[:SUBLANES])
        sh = jnp.concatenate([head, sh[SUBLANES:]], axis=0)
        acc = acc + cw[FFN_CONV - 1 - back:FFN_CONV - back, :] * sh
    o_ref[...] = (_silu(acc[:, :tn]) * acc[:, tn:]).astype(o_ref.dtype)
    ext_ref[...] = up[tm - SUBLANES:, :]


def ffn_up_call(h, w_up, conv_w, w_down, layer, tm=1024, tn=512):
    L, D = h.shape
    nb = FFN_DIM // tn
    nm = L // tm
    slab = FFN_DIM // (nb * nm)
    assert slab * nb * nm == FFN_DIM and slab % (2 * SUBLANES) == 0
    kern = functools.partial(_ffn_up_kernel, tm=tm, tn=tn)
    return pl.pallas_call(
        kern,
        grid=(nb, nm),
        in_specs=[pl.BlockSpec((tm, D), lambda n, m: (m, 0)),
                  pl.BlockSpec((None, D, tn), lambda n, m: (layer, 0, n)),
                  pl.BlockSpec((None, D, tn), lambda n, m: (layer, 0, n + nb)),
                  pl.BlockSpec((FFN_CONV, tn), lambda n, m: (0, n)),
                  pl.BlockSpec((FFN_CONV, tn), lambda n, m: (0, n + nb)),
                  pl.BlockSpec((None, slab, D), lambda n, m: (layer, n * nm + m, 0))],
        out_specs=[pl.BlockSpec((tm, tn), lambda n, m: (m, n)),
                   pl.BlockSpec((slab, D), lambda n, m: (n * nm + m, 0))],
        out_shape=[jax.ShapeDtypeStruct((L, FFN_DIM), BF16),
                   jax.ShapeDtypeStruct((FFN_DIM, D), BF16)],
        scratch_shapes=[pltpu.VMEM((SUBLANES, 2 * tn), F32),
                        pltpu.VMEM((D, 2 * tn), BF16)],
        compiler_params=_cparams(("parallel", "arbitrary")),
        name="ffn_up",
    )(h, w_up, w_up, conv_w, conv_w, w_down)


def _ffn_down_kernel(a_ref, w_ref, x_ref, nw_ref, *refs, last):
    acc_ref = refs[0]
    k = pl.program_id(1)

    @pl.when(k == 0)
    def _():
        acc_ref[...] = x_ref[...] + jnp.dot(a_ref[...], w_ref[...], preferred_element_type=F32)

    @pl.when(k > 0)
    def _():
        acc_ref[...] += jnp.dot(a_ref[...], w_ref[...], preferred_element_type=F32)

    @pl.when(k == pl.num_programs(1) - 1)
    def _():
        xo = acc_ref[...]
        if last:
            acc_ref[...] = _rms_rows(xo, nw_ref[...])
        else:
            refs[1][...] = _rms_rows(xo, nw_ref[...]).astype(refs[1].dtype)


def ffn_down_call(a, w_down, x, norm_w, last, tm=1024, tk=512):
    L, D = x.shape
    K = a.shape[1]
    row = pl.BlockSpec((tm, D), lambda m, k: (m, 0))
    if last:
        out_specs = [row]
        out_shape = [jax.ShapeDtypeStruct((L, D), F32)]
    else:
        out_specs = [row, row]
        out_shape = [jax.ShapeDtypeStruct((L, D), F32), jax.ShapeDtypeStruct((L, D), BF16)]
    kern = functools.partial(_ffn_down_kernel, last=last)
    return pl.pallas_call(
        kern,
        grid=(L // tm, K // tk),
        in_specs=[pl.BlockSpec((tm, tk), lambda m, k: (m, k)),
                  pl.BlockSpec((tk, D), lambda m, k: (k, 0)),
                  row,
                  pl.BlockSpec((1, D), lambda m, k: (0, 0))],
        out_specs=out_specs,
        out_shape=out_shape,
        compiler_params=_cparams(("parallel", "arbitrary")),
        name="ffn_down",
    )(a, w_down, x, norm_w.reshape(1, D))


def kernel(x, mix_norm_w, w_in, s5_log_dt, s5_a_re, s5_a_im, s5_b_re, s5_b_im, s5_c_re, s5_c_im, s5_d, s5_glu_w, dn_conv_w, dn_a_log, dn_dt_bias, dn_norm_w, dn_proj_w, w_out, ffn_norm_w, ffn_up, ffn_conv_w, ffn_down, final_norm_w):
    B, L, D = x.shape
    depth = w_in.shape[0]
    assert B == 1 and D == D_MODEL
    xs = x.reshape(L, D)
    h = rmsnorm_cast(xs, mix_norm_w[0])
    abr, abi, bbr, bbi = s5_discretise(s5_log_dt, s5_a_re, s5_a_im, s5_b_re, s5_b_im)
    s5_ar, s5_ai, s5_wb, s5_wc = s5_layout_params(abr, abi, bbr, bbi, s5_c_re, s5_c_im, depth)
    s5_ti = 32
    perm, perm_t = s5_perm(s5_ti)
    w_in_t = jnp.swapaxes(w_in, 1, 2)
    for l in range(depth):
        proj = in_proj(h, w_in_t, l)

        y_s5 = s5_scan(proj, perm, perm_t, s5_ar, s5_ai, s5_wb, s5_wc, s5_d, l, ti=s5_ti)
        y_dn = deltanet(proj, h, w_in_t, l, dn_conv_w[l], dn_a_log[l], dn_dt_bias[l], dn_norm_w[l])

        merged = merge(y_s5, y_dn, s5_glu_w, dn_proj_w, proj, l)
        xs, h = out_proj(merged, w_out, xs, ffn_norm_w[l], l)
        act, w_down = ffn_up_call(h, ffn_up, ffn_conv_w[l], ffn_down, l)
        if l + 1 < depth:
            xs, h = ffn_down_call(act, w_down, xs, mix_norm_w[l + 1], last=False)
        else:
            (out,) = ffn_down_call(act, w_down, xs, final_norm_w, last=True)
    return out.reshape(B, L, D)
```
